```python
import math
import jax, jax.numpy as jnp
from jax import lax
import numpy as np

D_MODEL = 1024
BATCH = 32
SEQ = 2048
DEPTH = 1
DEC_BATCH = 128
DEC_SEQ = 8
PAST_LEN = 8192
PAGE_SIZE = 128

HG_HEADS = 4
HG_DK = 128
HG_DV = 128
HG_F = HG_HEADS * HG_DK
HG_V = HG_HEADS * HG_DV
HG_CHUNK = 32
MB_HEADS = 8
MB_HD = 64
MB_W = MB_HEADS * MB_HD
MB_BLOCK = 256
MB_TOPK = 3
MB_QCHUNK = 8
N_BUCKETS = 32
MAX_EXACT = N_BUCKETS // 2
MAX_DISTANCE = 128
D_FF = 2816
N_IN = 2 * HG_F + 2 * HG_V + 3 * MB_W + 2 * D_MODEL
EPS = 1e-6
NEG = -1e30

kernel_name = "hgrn2_moba_macaron_decode_step"


def _rms(x, g):
    xf = x.astype(jnp.float32)
    y = xf * lax.rsqrt(jnp.mean(xf * xf, axis=-1, keepdims=True) + EPS)
    return (y * g.astype(jnp.float32)).astype(x.dtype)


def _swiglu(h, wg, wu, wd):
    return (jax.nn.silu(h @ wg) * (h @ wu)) @ wd


def _rel_bucket(dist):
    n = jnp.maximum(dist, 0)
    nf = jnp.maximum(n, 1).astype(jnp.float32)
    large = MAX_EXACT + (jnp.log(nf / MAX_EXACT) / math.log(MAX_DISTANCE / MAX_EXACT)
                         * (N_BUCKETS - MAX_EXACT)).astype(jnp.int32)
    large = jnp.minimum(large, N_BUCKETS - 1)
    return jnp.where(n < MAX_EXACT, n, large)


def _branch_inputs(h, w_in_l):
    B, L, _ = h.shape
    sizes = (HG_F, HG_F, HG_V, HG_V, MB_W, MB_W, MB_W, D_MODEL, D_MODEL)
    cuts = [int(c) for c in np.cumsum(sizes)[:-1]]
    hq, hf, hv, hg, mq, mk, mv, ga, gb = jnp.split(h @ w_in_l, cuts, axis=-1)
    return (hq.reshape(B, L, HG_HEADS, HG_DK), hf.reshape(B, L, HG_HEADS, HG_DK),
            hv.reshape(B, L, HG_HEADS, HG_DV), hg,
            mq.reshape(B, L, MB_HEADS, MB_HD), mk.reshape(B, L, MB_HEADS, MB_HD),
            mv.reshape(B, L, MB_HEADS, MB_HD), ga, gb)


def _hgrn2(q, f_pre, v, s0, lb):
    B, L = q.shape[:2]
    c = HG_CHUNK if L % HG_CHUNK == 0 else L
    n = L // c
    fp = f_pre.astype(jnp.float32)
    logf = jnp.log(lb + (1.0 - lb) * jax.nn.sigmoid(fp))
    kk = (1.0 - lb) * jax.nn.sigmoid(-fp)
    qf = jax.nn.silu(q.astype(jnp.float32))

    def to_chunks(a):
        return a.reshape(B, n, c, a.shape[2], a.shape[3]).transpose(1, 0, 3, 2, 4)

    mask = jnp.tril(jnp.ones((c, c), bool))[:, :, None]

    def step(S, xs):
        qc, kc, vc, gc = xs
        b = jnp.cumsum(gc, axis=2)
        o_inter = jnp.einsum('bhtk,bhkv->bhtv', qc * jnp.exp(b), S)
        diff = b[:, :, :, None, :] - b[:, :, None, :, :]
        decay = jnp.exp(jnp.where(mask, diff, -jnp.inf))
        att = jnp.einsum('bhtk,bhsk,bhtsk->bhts', qc, kc, decay)
        o = o_inter + jnp.einsum('bhts,bhsv->bhtv', att, vc)
        b_last = b[:, :, -1:, :]
        S_new = jnp.exp(b_last[:, :, 0, :])[..., None] * S + jnp.einsum(
            'bhsk,bhsv->bhkv', kc * jnp.exp(b_last - b), vc)
        return S_new, o

    s_fin, o = lax.scan(step, s0.astype(jnp.float32),
                        (to_chunks(qf), to_chunks(kk), to_chunks(v.astype(jnp.float32)), to_chunks(logf)))
    o = o.transpose(1, 0, 3, 2, 4).reshape(B, L, HG_HEADS, HG_DV)
    return o, s_fin


def _hgrn_out(o, g, norm_g):
    B, L = o.shape[:2]
    o = o * lax.rsqrt(jnp.mean(o * o, axis=-1, keepdims=True) + EPS) * norm_g.astype(jnp.float32)
    return (o.reshape(B, L, HG_V) * jax.nn.silu(g.astype(jnp.float32))).astype(g.dtype)


def _moba_softmax(q, k_own, v_own, bias_own, mask_own, k_sel=None, v_sel=None, bias_sel=None, mask_sel=None):
    qf = q.astype(jnp.float32) * (MB_HD ** -0.5)
    s_own = jnp.einsum('bhqd,bhkd->bhqk', qf, k_own.astype(jnp.float32)) + bias_own
    s_own = jnp.where(mask_own, s_own, NEG)
    if k_sel is None:
        p = jax.nn.softmax(s_own, axis=-1)
        out = jnp.einsum('bhqk,bhkd->bhqd', p, v_own.astype(jnp.float32))
        return out.astype(q.dtype)
    s_sel = jnp.einsum('bhqd,bhqkd->bhqk', qf, k_sel.astype(jnp.float32)) + bias_sel
    if mask_sel is not None:
        s_sel = jnp.where(mask_sel, s_sel, NEG)
    n_s = s_sel.shape[-1]
    p = jax.nn.softmax(jnp.concatenate([s_sel, s_own], axis=-1), axis=-1)
    out = (jnp.einsum('bhqk,bhqkd->bhqd', p[..., :n_s], v_sel.astype(jnp.float32))
           + jnp.einsum('bhqk,bhkd->bhqd', p[..., n_s:], v_own.astype(jnp.float32)))
    return out.astype(q.dtype)


def _moba_prompt(q, k, v, rel_bias):
    B, S = q.shape[:2]
    nb = -(-S // MB_BLOCK)
    pad = nb * MB_BLOCK - S
    qh = q.transpose(0, 2, 1, 3)
    kb = jnp.pad(k.transpose(0, 2, 1, 3), ((0, 0), (0, 0), (0, pad), (0, 0))).reshape(B, MB_HEADS, nb, MB_BLOCK, MB_HD)
    vb = jnp.pad(v.transpose(0, 2, 1, 3), ((0, 0), (0, 0), (0, pad), (0, 0))).reshape(B, MB_HEADS, nb, MB_BLOCK, MB_HD)
    n_sel = min(MB_TOPK, nb - 1)
    nq = S // MB_QCHUNK
    q_c = qh.reshape(B, MB_HEADS, nq, MB_QCHUNK, MB_HD).transpose(2, 0, 1, 3, 4)
    bi = jnp.arange(B)[:, None, None, None]
    hi4 = jnp.arange(MB_HEADS)[None, :, None, None]
    hi5 = jnp.arange(MB_HEADS)[None, :, None, None, None]
    rel_t = rel_bias.T
    if n_sel > 0:
        kmean = kb.astype(jnp.float32).mean(axis=3)
        gs = jnp.einsum('bhsd,bhnd->bhsn', qh.astype(jnp.float32), kmean)
        past = jnp.arange(nb)[None, :] < (jnp.arange(S) // MB_BLOCK)[:, None]
        gs = jnp.where(past[None, None], gs, NEG)
        _, sel = lax.top_k(gs, n_sel)
        sel_c = sel.reshape(B, MB_HEADS, nq, MB_QCHUNK, n_sel).transpose(2, 0, 1, 3, 4)
        xs = (jnp.arange(nq), q_c, sel_c)
    else:
        xs = (jnp.arange(nq), q_c)

    def body(args):
        ci, qc = args[0], args[1]
        qpos = ci * MB_QCHUNK + jnp.arange(MB_QCHUNK)
        ob = (ci * MB_QCHUNK) // MB_BLOCK
        k_own = lax.dynamic_index_in_dim(kb, ob, axis=2, keepdims=False)
        v_own = lax.dynamic_index_in_dim(vb, ob, axis=2, keepdims=False)
        dist_own = qpos[:, None] - (ob * MB_BLOCK + jnp.arange(MB_BLOCK))[None, :]
        bias_own = rel_bias[_rel_bucket(dist_own)].transpose(2, 0, 1)[None]
        mask_own = (dist_own >= 0)[None, None]
        if n_sel == 0:
            return _moba_softmax(qc, k_own, v_own, bias_own, mask_own)
        selc = args[2]
        k_sel = kb[bi, hi4, selc].reshape(B, MB_HEADS, MB_QCHUNK, n_sel * MB_BLOCK, MB_HD)
        v_sel = vb[bi, hi4, selc].reshape(B, MB_HEADS, MB_QCHUNK, n_sel * MB_BLOCK, MB_HD)
        dist_sel = qpos[:, None, None] - (selc[..., None] * MB_BLOCK + jnp.arange(MB_BLOCK))
        bias_sel = rel_t[hi5, _rel_bucket(dist_sel)].reshape(B, MB_HEADS, MB_QCHUNK, n_sel * MB_BLOCK)
        valid = jnp.arange(n_sel)[None, :] < (qpos // MB_BLOCK)[:, None]
        mask_sel = jnp.repeat(valid, MB_BLOCK, axis=-1)[None, None]
        return _moba_softmax(qc, k_own, v_own, bias_own, mask_own, k_sel, v_sel, bias_sel, mask_sel)

    out = lax.map(body, xs)
    out = out.transpose(1, 0, 3, 2, 4).reshape(B, S, MB_W)
    return out


def _moba_sample(q, k, v, ck, cv, page_table, rel_bias):
    DB, T = q.shape[:2]
    n_full = PAST_LEN // MB_BLOCK
    rem = PAST_LEN - n_full * MB_BLOCK
    ppb = MB_BLOCK // PAGE_SIZE
    n_sel = min(MB_TOPK, n_full)
    qh, kh, vh = (a.transpose(0, 2, 1, 3) for a in (q, k, v))
    own_start = PAST_LEN - rem
    if rem > 0:
        own_pages = page_table[:, own_start // PAGE_SIZE:]

        def rows(c):
            r = c[own_pages]
            return r.transpose(0, 2, 1, 3, 4).reshape(DB, MB_HEADS, rem, MB_HD).astype(q.dtype)
        k_own = jnp.concatenate([rows(ck), kh], axis=2)
        v_own = jnp.concatenate([rows(cv), vh], axis=2)
    else:
        k_own, v_own = kh, vh
    own_pos = jnp.arange(own_start, PAST_LEN + T)
    hi = jnp.arange(MB_HEADS)
    rel_t = rel_bias.T
    if n_sel > 0:
        page_sum = ck.sum(axis=2, dtype=jnp.float32)
        blk = page_sum[page_table[:, :n_full * ppb]].reshape(DB, n_full, ppb, MB_HEADS, MB_HD).sum(2) / MB_BLOCK
        gs = jnp.einsum('bhtd,bnhd->bhtn', qh.astype(jnp.float32), blk)
        _, sel = lax.top_k(gs, n_sel)
        pg_idx = sel[..., None] * ppb + jnp.arange(ppb)
        phys = page_table[jnp.arange(DB)[:, None, None, None, None], pg_idx]
        xs = (jnp.arange(T), qh.transpose(2, 0, 1, 3), sel.transpose(2, 0, 1, 3), phys.transpose(2, 0, 1, 3, 4))
    else:
        xs = (jnp.arange(T), qh.transpose(2, 0, 1, 3))

    def body(args):
        t, qt = args[0], args[1][:, :, None, :]
        pos = PAST_LEN + t
        dist_own = pos - own_pos
        bias_own = rel_bias[_rel_bucket(dist_own)].T[None, :, None, :]
        mask_own = (dist_own >= 0)[None, None, None, :]
        if n_sel == 0:
            return _moba_softmax(qt, k_own, v_own, bias_own, mask_own)
        selt, physt = args[2], args[3]
        hsel = hi[None, :, None, None]
        k_sel = ck[physt, hsel].reshape(DB, MB_HEADS, 1, n_sel * MB_BLOCK, MB_HD)
        v_sel = cv[physt, hsel].reshape(DB, MB_HEADS, 1, n_sel * MB_BLOCK, MB_HD)
        dist_sel = pos - (selt[..., None] * MB_BLOCK + jnp.arange(MB_BLOCK))
        bias_sel = rel_t[hi[None, :, None, None], _rel_bucket(dist_sel)].reshape(DB, MB_HEADS, 1, n_sel * MB_BLOCK)
        return _moba_softmax(qt, k_own, v_own, bias_own, mask_own, k_sel, v_sel, bias_sel, None)

    out = lax.map(body, xs)
    return out[:, :, :, 0, :].transpose(1, 0, 2, 3).reshape(DB, T, MB_W)


def _merge(ya, yb, ga, gb, wa, wb, wo):
    m = jax.nn.sigmoid(ga) * (ya @ wa) + jax.nn.sigmoid(gb) * (yb @ wb)
    return m @ wo


def setup_inputs(seed: int = 0) -> dict:
    key = jax.random.key(seed)
    ks = jax.random.split(key, 24)
    f32 = jnp.float32
    n_pages = PAST_LEN // PAGE_SIZE
    n_pool = (DEC_BATCH * n_pages * 5) // 4

    def dense(k, shape, fan_in):
        return jax.random.normal(k, shape, f32) * (fan_in ** -0.5)

    def gain(k, shape):
        return 1.0 + 0.05 * jax.random.normal(k, shape, f32)

    page_table = jax.random.permutation(ks[5], n_pool)[:DEC_BATCH * n_pages].reshape(DEC_BATCH, n_pages).astype(jnp.int32)
    return {
        'x_prompt': jax.random.normal(ks[0], (BATCH, SEQ, D_MODEL), f32),
        'x_sample': jax.random.normal(ks[1], (DEC_BATCH, DEC_SEQ, D_MODEL), f32),
        'state_hgrn': 0.5 * jax.random.normal(ks[2], (DEPTH, DEC_BATCH, HG_HEADS, HG_DK, HG_DV), f32),
        'cache_k': jax.random.normal(ks[3], (DEPTH, n_pool, MB_HEADS, PAGE_SIZE, MB_HD), f32),
        'cache_v': jax.random.normal(ks[4], (DEPTH, n_pool, MB_HEADS, PAGE_SIZE, MB_HD), f32),
        'page_table': page_table,
        'ffn1_norm': gain(ks[6], (DEPTH, D_MODEL)),
        'ffn1_w_gate': dense(ks[7], (DEPTH, D_MODEL, D_FF), D_MODEL),
        'ffn1_w_up': dense(ks[8], (DEPTH, D_MODEL, D_FF), D_MODEL),
        'ffn1_w_down': dense(ks[9], (DEPTH, D_FF, D_MODEL), D_FF),
        'mix_norm': gain(ks[10], (DEPTH, D_MODEL)),
        'w_in': dense(ks[11], (DEPTH, D_MODEL, N_IN), D_MODEL),
        'hg_lb': 0.5 * jax.random.normal(ks[12], (DEPTH + 1, HG_F), f32),
        'hg_norm': gain(ks[13], (DEPTH, HG_DV)),
        'rel_bias': 0.5 * jax.random.normal(ks[14], (N_BUCKETS, MB_HEADS), f32),
        'w_branch_a': dense(ks[15], (DEPTH, HG_V, D_MODEL), HG_V),
        'w_branch_b': dense(ks[16], (DEPTH, MB_W, D_MODEL), MB_W),
        'w_out': dense(ks[17], (DEPTH, D_MODEL, D_MODEL), D_MODEL),
        'ffn2_norm': gain(ks[18], (DEPTH, D_MODEL)),
        'ffn2_w_gate': dense(ks[19], (DEPTH, D_MODEL, D_FF), D_MODEL),
        'ffn2_w_up': dense(ks[20], (DEPTH, D_MODEL, D_FF), D_MODEL),
        'ffn2_w_down': dense(ks[21], (DEPTH, D_FF, D_MODEL), D_FF),
        'final_norm': gain(ks[22], (D_MODEL,)),
    }


def reference(x_prompt, x_sample, state_hgrn, cache_k, cache_v, page_table,
              ffn1_norm, ffn1_w_gate, ffn1_w_up, ffn1_w_down,
              mix_norm, w_in, hg_lb, hg_norm, rel_bias, w_branch_a, w_branch_b, w_out,
              ffn2_norm, ffn2_w_gate, ffn2_w_up, ffn2_w_down, final_norm):
    lbs = jnp.cumsum(jax.nn.softmax(hg_lb.astype(jnp.float32), axis=0), axis=0)
    xp, xs = x_prompt, x_sample
    sp_l, ss_l, kp_l, vp_l, ks_l, vs_l = [], [], [], [], [], []
    for l in range(DEPTH):
        lb = lbs[l].reshape(HG_HEADS, HG_DK)
        xp = xp + 0.5 * _swiglu(_rms(xp, ffn1_norm[l]), ffn1_w_gate[l], ffn1_w_up[l], ffn1_w_down[l])
        xs = xs + 0.5 * _swiglu(_rms(xs, ffn1_norm[l]), ffn1_w_gate[l], ffn1_w_up[l], ffn1_w_down[l])
        pq, pf, pv, pg, pmq, pmk, pmv, pga, pgb = _branch_inputs(_rms(xp, mix_norm[l]), w_in[l])
        sq, sf, sv, sg, smq, smk, smv, sga, sgb = _branch_inputs(_rms(xs, mix_norm[l]), w_in[l])
        s0 = jnp.zeros((xp.shape[0], HG_HEADS, HG_DK, HG_DV), jnp.float32)
        o_p, st_p = _hgrn2(pq, pf, pv, s0, lb)
        o_s, st_s = _hgrn2(sq, sf, sv, state_hgrn[l], lb)
        ya_p = _hgrn_out(o_p, pg, hg_norm[l])
        ya_s = _hgrn_out(o_s, sg, hg_norm[l])
        yb_p = _moba_prompt(pmq, pmk, pmv, rel_bias)
        yb_s = _moba_sample(smq, smk, smv, cache_k[l], cache_v[l], page_table, rel_bias)
        xp = xp + _merge(ya_p, yb_p, pga, pgb, w_branch_a[l], w_branch_b[l], w_out[l])
        xs = xs + _merge(ya_s, yb_s, sga, sgb, w_branch_a[l], w_branch_b[l], w_out[l])
        xp = xp + 0.5 * _swiglu(_rms(xp, ffn2_norm[l]), ffn2_w_gate[l], ffn2_w_up[l], ffn2_w_down[l])
        xs = xs + 0.5 * _swiglu(_rms(xs, ffn2_norm[l]), ffn2_w_gate[l], ffn2_w_up[l], ffn2_w_down[l])
        sp_l.append(st_p.astype(x_prompt.dtype))
        ss_l.append(st_s.astype(state_hgrn.dtype))
        kp_l.append(pmk.transpose(0, 2, 1, 3))
        vp_l.append(pmv.transpose(0, 2, 1, 3))
        ks_l.append(smk.transpose(0, 2, 1, 3))
        vs_l.append(smv.transpose(0, 2, 1, 3))
    y_prompt = _rms(xp, final_norm)
    y_sample = _rms(xs, final_norm)
    state_hgrn_prompt = jnp.stack(sp_l)
    state_hgrn_sample = jnp.stack(ss_l)
    k_prompt = jnp.stack(kp_l)
    v_prompt = jnp.stack(vp_l)
    k_sample = jnp.stack(ks_l)
    v_sample = jnp.stack(vs_l)
    return (y_prompt, y_sample, state_hgrn_prompt, state_hgrn_sample, k_prompt, v_prompt, k_sample, v_sample)
```

```python
import functools
import math

import numpy as np
import jax
import jax.numpy as jnp
from jax import lax
from jax.experimental import pallas as pl
from jax.experimental.pallas import tpu as pltpu

F32 = jnp.float32
BF16 = jnp.bfloat16

EPS = 1e-6
NEG = -1e30
HG_HEADS = 4
HG_DK = 128
HG_DV = 128
MB_HEADS = 8
MB_HD = 64
MB_BLOCK = 256
MB_TOPK = 3
PAGE_SIZE = 128
N_BUCKETS = 32
MAX_EXACT = N_BUCKETS // 2
MAX_DISTANCE = 128

V7X_VMEM_BYTES = 64 * 2**20
VMEM_BIG = V7X_VMEM_BYTES - 8 * 2**20
VMEM_SMALL = 40 * 2**20

ROW_TILE = 256
HG_CHUNK = 64
HG_STEP = 512
PAGES_PER_STEP = 8

_NT = (((1,), (1,)), ((), ()))


def _dot(a, b):
    return jnp.dot(a, b, preferred_element_type=F32)


def _dot_nt(a, b):
    return lax.dot_general(a, b, _NT, preferred_element_type=F32)


def _split2(a):
    hi = a.astype(BF16)
    lo = (a - hi.astype(F32)).astype(BF16)
    return hi, lo


def _dot_precise(a, b):
    a1, a2 = _split2(a)
    b1, b2 = _split2(b)
    return _dot(a1, b1) + _dot(a1, b2) + _dot(a2, b1)


def _rms(x, g):
    return x * lax.rsqrt(jnp.mean(x * x, axis=-1, keepdims=True) + EPS) * g


def _swiglu(h, wg_ref, wu_ref, wd_ref, ff_chunk):
    d_ff = wg_ref.shape[1]
    acc = None
    for c0 in range(0, d_ff, ff_chunk):
        a = _dot(h, wg_ref[:, c0:c0 + ff_chunk])
        u = _dot(h, wu_ref[:, c0:c0 + ff_chunk])
        act = (a * jax.nn.sigmoid(a) * u).astype(BF16)
        part = _dot(act, wd_ref[c0:c0 + ff_chunk, :])
        acc = part if acc is None else acc + part
    return acc


def _ff_chunk(d_ff):
    half = d_ff // 2
    return half if (d_ff % 2 == 0 and half % 128 == 0) else d_ff


def _const_spec(shape):
    nd = len(shape)
    return pl.BlockSpec(shape, lambda *_: (0,) * nd, pipeline_mode=pl.Buffered(1))


def _ffn_in_kernel(x_ref, n1_ref, wg_ref, wu_ref, wd_ref, nm_ref, wz_ref, wkt_ref, wvt_ref, wgab_ref,
                   x1_ref, zh_ref, mq_ref, kt_ref, vt_ref, gab_ref, *, ff_chunk, n_hg):
    x = x_ref[0]
    h = _rms(x, n1_ref[...]).astype(BF16)
    x1 = x + 0.5 * _swiglu(h, wg_ref, wu_ref, wd_ref, ff_chunk)
    x1_ref[0] = x1
    h2 = _rms(x1, nm_ref[...]).astype(BF16)
    z = _dot(h2, wz_ref[...])
    zh_ref[0] = z[:, :n_hg]
    mq_ref[0] = z[:, n_hg:]
    kt_ref[0] = _dot_nt(wkt_ref[...], h2)
    vt_ref[0] = _dot_nt(wvt_ref[...], h2)
    gab_ref[0] = _dot(h2, wgab_ref[...]).astype(BF16)


def _ffn_in(x, n1, wg, wu, wd, nm, wz, wkt, wvt, wgab):
    bsz, seq, d = x.shape
    tm = min(ROW_TILE, seq)
    assert seq % tm == 0
    n_z = wz.shape[1]
    n_hg = 2 * HG_HEADS * HG_DK + 2 * HG_HEADS * HG_DV
    n_kv = wkt.shape[0]
    n_gab = wgab.shape[1]
    row = lambda w: pl.BlockSpec((1, tm, w), lambda b, i: (b, i, 0))
    col = pl.BlockSpec((1, n_kv, tm), lambda b, i: (b, 0, i))
    outs = pl.pallas_call(
        functools.partial(_ffn_in_kernel, ff_chunk=_ff_chunk(wg.shape[1]), n_hg=n_hg),
        grid=(bsz, seq // tm),
        in_specs=[row(d), _const_spec(n1.shape), _const_spec(wg.shape), _const_spec(wu.shape),
                  _const_spec(wd.shape), _const_spec(nm.shape), _const_spec(wz.shape),
                  _const_spec(wkt.shape), _const_spec(wvt.shape), _const_spec(wgab.shape)],
        out_specs=[row(d), row(n_hg), row(n_z - n_hg), col, col, row(n_gab)],
        out_shape=[jax.ShapeDtypeStruct((bsz, seq, d), F32),
                   jax.ShapeDtypeStruct((bsz, seq, n_hg), F32),
                   jax.ShapeDtypeStruct((bsz, seq, n_z - n_hg), F32),
                   jax.ShapeDtypeStruct((bsz, n_kv, seq), F32),
                   jax.ShapeDtypeStruct((bsz, n_kv, seq), F32),
                   jax.ShapeDtypeStruct((bsz, seq, n_gab), BF16)],
        compiler_params=pltpu.CompilerParams(dimension_semantics=("parallel", "parallel"),
                                             vmem_limit_bytes=VMEM_BIG),
        name="ffn_in",
    )(x, n1, wg, wu, wd, nm, wz, wkt, wvt, wgab)
    return outs


def _merge_ffn_kernel(x1_ref, ya_ref, yb_ref, gab_ref, wa_ref, wb_ref, wo_ref, n2_ref,
                      wg_ref, wu_ref, wd_ref, nf_ref, y_ref, *, ff_chunk):
    d = x1_ref.shape[-1]
    gab = gab_ref[0].astype(F32)
    ma = _dot(ya_ref[0].astype(BF16), wa_ref[...])
    mb = _dot(yb_ref[0].astype(BF16), wb_ref[...])
    m = jax.nn.sigmoid(gab[:, :d]) * ma + jax.nn.sigmoid(gab[:, d:]) * mb
    x2 = x1_ref[0] + _dot(m.astype(BF16), wo_ref[...])
    h = _rms(x2, n2_ref[...]).astype(BF16)
    x3 = x2 + 0.5 * _swiglu(h, wg_ref, wu_ref, wd_ref, ff_chunk)
    y_ref[0] = _rms(x3, nf_ref[...])


def _merge_ffn(x1, ya, yb, gab, wa, wb, wo, n2, wg, wu, wd, nf):
    bsz, seq, d = x1.shape
    tm = min(ROW_TILE, seq)
    assert seq % tm == 0
    row = lambda w: pl.BlockSpec((1, tm, w), lambda b, i: (b, i, 0))
    return pl.pallas_call(
        functools.partial(_merge_ffn_kernel, ff_chunk=_ff_chunk(wg.shape[1])),
        grid=(bsz, seq // tm),
        in_specs=[row(d), row(ya.shape[-1]), row(yb.shape[-1]), row(gab.shape[-1]),
                  _const_spec(wa.shape), _const_spec(wb.shape), _const_spec(wo.shape), _const_spec(n2.shape),
                  _const_spec(wg.shape), _const_spec(wu.shape), _const_spec(wd.shape), _const_spec(nf.shape)],
        out_specs=row(d),
        out_shape=jax.ShapeDtypeStruct((bsz, seq, d), F32),
        compiler_params=pltpu.CompilerParams(dimension_semantics=("parallel", "parallel"),
                                             vmem_limit_bytes=VMEM_BIG),
        name="merge_ffn",
    )(x1, ya, yb, gab, wa, wb, wo, n2, wg, wu, wd, nf)


def _hgrn_levels(chunk):
    levels, m = [], 1
    while m < chunk:
        levels.append(m)
        m *= 2
    return levels


def _hgrn_decay_matrix(chunk):
    t = np.arange(chunk)[:, None]
    u = np.arange(chunk)[None, :]
    mats = [u <= t, u > t]
    for m in _hgrn_levels(chunk):
        mid_q = t & ~(m - 1)
        mats.append(((t & m) != 0) & (u >= mid_q) & (u <= t))
        mid_k = (t & ~(2 * m - 1)) + m
        mats.append(((t & m) == 0) & (u > t) & (u < mid_k))
    return np.concatenate(mats, axis=0).astype(np.float32)


def _hgrn_kernel(zh_ref, s0_ref, lb_ref, ng_ref, dm_ref, ya_ref, sout_ref, st_ref,
                 *, chunk, n_inner, valid_len):
    nh, dk, dv = HG_HEADS, HG_DK, HG_DV
    nf = nh * dk
    j = pl.program_id(1)

    @pl.when(j == 0)
    def _():
        for h in range(nh):
            st_ref[h] = s0_ref[0, h].T

    lb = lb_ref[...]
    ng = ng_ref[...]
    dmat = dm_ref[...]
    ti = lax.broadcasted_iota(jnp.int32, (chunk, chunk), 0)
    si = lax.broadcasted_iota(jnp.int32, (chunk, chunk), 1)
    xs = jnp.where(ti > si, jnp.bitwise_xor(ti, si), 0)
    levels = _hgrn_levels(chunk)

    def one_chunk(ci, carry):
        r0 = pl.multiple_of(ci * chunk, chunk)
        zh = zh_ref[0, pl.ds(r0, chunk), :]
        hq, hf = zh[:, :nf], zh[:, nf:2 * nf]
        hv, hg = zh[:, 2 * nf:2 * nf + nh * dv], zh[:, 2 * nf + nh * dv:]
        logf = jnp.log(lb + (1.0 - lb) * jax.nn.sigmoid(hf))
        kk = (1.0 - lb) * jax.nn.sigmoid(-hf)
        if valid_len < chunk:
            live = lax.broadcasted_iota(jnp.int32, (chunk, 1), 0) < valid_len
            logf = jnp.where(live, logf, 0.0)
            kk = jnp.where(live, kk, 0.0)
        qf = hq * jax.nn.sigmoid(hq)
        g1 = logf.astype(BF16)
        r1 = logf - g1.astype(F32)
        g2 = r1.astype(BF16)
        g3 = (r1 - g2.astype(F32)).astype(BF16)
        dec = _dot(dmat, g1) + _dot(dmat, g2) + _dot(dmat, g3)
        outs = []
        for h in range(nh):
            ks = slice(h * dk, (h + 1) * dk)
            vs = slice(h * dv, (h + 1) * dv)
            q_h, k_h, v_h = qf[:, ks], kk[:, ks], hv[:, vs]
            b = dec[0:chunk, ks]
            after = dec[chunk:2 * chunk, ks]
            st = st_ref[h]
            o = _dot_nt((q_h * jnp.exp(b)).astype(BF16), st.astype(BF16))
            att = None
            for li, m in enumerate(levels):
                dq = dec[(2 + 2 * li) * chunk:(3 + 2 * li) * chunk, ks]
                dkk = dec[(3 + 2 * li) * chunk:(4 + 2 * li) * chunk, ks]
                a_l = _dot_nt((q_h * jnp.exp(dq)).astype(BF16), (k_h * jnp.exp(dkk)).astype(BF16))
                att = a_l if att is None else jnp.where(xs >= m, a_l, att)
            diag = jnp.sum(q_h * k_h, axis=-1, keepdims=True)
            att = jnp.where(xs >= 1, att, 0.0)
            att = jnp.where(ti == si, diag, att)
            o = o + _dot(att.astype(BF16), v_h.astype(BF16))
            kd = (k_h * jnp.exp(after)).astype(BF16)
            st_ref[h] = jnp.exp(b[chunk - 1:chunk, :]) * st + _dot(v_h.T.astype(BF16), kd)
            o = o * lax.rsqrt(jnp.mean(o * o, axis=-1, keepdims=True) + EPS) * ng
            g_h = hg[:, vs]
            outs.append((o * (g_h * jax.nn.sigmoid(g_h))).astype(BF16))
        ya_ref[0, pl.ds(r0, chunk), :] = jnp.concatenate(outs, axis=-1)
        return carry

    lax.fori_loop(0, n_inner, one_chunk, 0)

    @pl.when(j == pl.num_programs(1) - 1)
    def _():
        for h in range(nh):
            sout_ref[0, h] = st_ref[h].T


def _hgrn(zh, s0, lb, ng, valid_len):
    bsz, seq, width = zh.shape
    chunk = min(HG_CHUNK, seq)
    step = min(HG_STEP, seq)
    assert seq % step == 0 and step % chunk == 0 and chunk % 16 == 0
    dmat = jnp.asarray(_hgrn_decay_matrix(chunk), BF16)
    nv = HG_HEADS * HG_DV
    st_spec = pl.BlockSpec((1, HG_HEADS, HG_DK, HG_DV), lambda b, j: (b, 0, 0, 0))
    return pl.pallas_call(
        functools.partial(_hgrn_kernel, chunk=chunk, n_inner=step // chunk, valid_len=min(valid_len, chunk)),
        grid=(bsz, seq // step),
        in_specs=[pl.BlockSpec((1, step, width), lambda b, j: (b, j, 0)), st_spec,
                  _const_spec(lb.shape), _const_spec(ng.shape), _const_spec(dmat.shape)],
        out_specs=[pl.BlockSpec((1, step, nv), lambda b, j: (b, j, 0)), st_spec],
        out_shape=[jax.ShapeDtypeStruct((bsz, seq, nv), BF16),
                   jax.ShapeDtypeStruct((bsz, HG_HEADS, HG_DK, HG_DV), F32)],
        scratch_shapes=[pltpu.VMEM((HG_HEADS, HG_DV, HG_DK), F32)],
        compiler_params=pltpu.CompilerParams(dimension_semantics=("parallel", "arbitrary"),
                                             vmem_limit_bytes=VMEM_SMALL),
        name="hgrn",
    )(zh, s0, lb, ng, dmat)


def _rel_bucket(dist):
    n = jnp.maximum(dist, 0)
    nf = jnp.maximum(n, 1).astype(F32)
    large = MAX_EXACT + (jnp.log(nf / MAX_EXACT) / math.log(MAX_DISTANCE / MAX_EXACT)
                         * (N_BUCKETS - MAX_EXACT)).astype(jnp.int32)
    large = jnp.minimum(large, N_BUCKETS - 1)
    return jnp.where(n < MAX_EXACT, n, large)


def _top_mask(gs, lane, n_sel):
    mask = jnp.zeros(gs.shape, F32)
    lane_f = lane.astype(F32)
    cur = gs
    for _ in range(n_sel):
        best = jnp.max(cur, axis=-1, keepdims=True)
        idx = jnp.min(jnp.where(cur == best, lane_f, float(gs.shape[-1])), axis=-1, keepdims=True)
        hit = lane_f == idx
        mask = jnp.where(hit, 1.0, mask)
        cur = jnp.where(hit, -jnp.inf, cur)
    return mask


def _moba_prompt_kernel(q_ref, kt_ref, vt_ref, bias_ref, o_ref, *, n_blocks):
    blk, hd = MB_BLOCK, MB_HD
    heads = q_ref.shape[-1] // hd
    scale = hd ** -0.5
    lane = lax.broadcasted_iota(jnp.int32, (blk, 128), 1)
    ti = lax.broadcasted_iota(jnp.int32, (blk, blk), 0)
    si = lax.broadcasted_iota(jnp.int32, (blk, blk), 1)
    causal = ti >= si
    erow = lax.broadcasted_iota(jnp.int32, (128, blk), 0)

    qs, kts, vts, gss = [], [], [], []
    q_all = q_ref[0]
    for hl in range(heads):
        q = q_all[:, hl * hd:(hl + 1) * hd]
        kt = kt_ref[0, hl * hd:(hl + 1) * hd, :]
        qs.append((q * scale).astype(BF16))
        kts.append(kt.astype(BF16))
        vts.append(vt_ref[0, hl * hd:(hl + 1) * hd, :].astype(BF16))
        if n_blocks > 1:
            lane_k = lax.broadcasted_iota(jnp.int32, (hd, 128), 1)
            kmean = jnp.zeros((hd, 128), F32)
            for n in range(n_blocks):
                ksum = jnp.sum(kt[:, n * blk:(n + 1) * blk], axis=-1, keepdims=True)
                kmean = jnp.where(lane_k == n, ksum * (1.0 / blk), kmean)
            gss.append(_dot_precise(q, kmean))

    for i in range(n_blocks):
        rows = slice(i * blk, (i + 1) * blk)
        outs = []
        for hl in range(heads):
            qi = qs[hl][rows]
            n_sel = min(MB_TOPK, i)
            if n_sel > 0:
                gs = jnp.where(lane < i, gss[hl][rows], -jnp.inf)
                sel = _top_mask(gs, lane, n_sel).astype(BF16)
            tiles = []
            for jb in range(i + 1):
                cols = slice(jb * blk, (jb + 1) * blk)
                s = _dot(qi, kts[hl][:, cols]) + bias_ref[min(i - jb, 2), hl]
                if jb == i:
                    s = jnp.where(causal, s, NEG)
                else:
                    expand = jnp.where(erow == jb, 1.0, 0.0).astype(BF16)
                    s = jnp.where(_dot(sel, expand) > 0.5, s, NEG)
                tiles.append(s)
            mx = tiles[0].max(axis=-1, keepdims=True)
            for s in tiles[1:]:
                mx = jnp.maximum(mx, s.max(axis=-1, keepdims=True))
            den = None
            acc = None
            for jb, s in enumerate(tiles):
                p = jnp.exp(s - mx)
                ps = jnp.sum(p, axis=-1, keepdims=True)
                pv = _dot_nt(p.astype(BF16), vts[hl][:, jb * blk:(jb + 1) * blk])
                den = ps if den is None else den + ps
                acc = pv if acc is None else acc + pv
            outs.append((acc / den).astype(BF16))
        o_ref[0, rows, :] = jnp.concatenate(outs, axis=-1)


def _moba_prompt(mq, kt, vt, bias_tiles):
    bsz, seq, width = mq.shape
    assert seq % MB_BLOCK == 0 and MB_BLOCK >= MAX_DISTANCE
    hp = 128 // MB_HD
    n_groups = width // 128
    return pl.pallas_call(
        functools.partial(_moba_prompt_kernel, n_blocks=seq // MB_BLOCK),
        grid=(n_groups, bsz),
        in_specs=[pl.BlockSpec((1, seq, 128), lambda g, b: (b, 0, g)),
                  pl.BlockSpec((1, 128, seq), lambda g, b: (b, g, 0)),
                  pl.BlockSpec((1, 128, seq), lambda g, b: (b, g, 0)),
                  pl.BlockSpec((3, hp, MB_BLOCK, MB_BLOCK), lambda g, b: (0, g, 0, 0))],
        out_specs=pl.BlockSpec((1, seq, 128), lambda g, b: (b, 0, g)),
        out_shape=jax.ShapeDtypeStruct((bsz, seq, width), BF16),
        compiler_params=pltpu.CompilerParams(dimension_semantics=("parallel", "parallel"),
                                             vmem_limit_bytes=VMEM_SMALL),
        name="moba_prompt",
    )(mq, kt, vt, bias_tiles)


def _prompt_bias_tiles(rel_bias):
    t = jnp.arange(MB_BLOCK)[:, None]
    s = jnp.arange(MB_BLOCK)[None, :]
    tiles = [rel_bias[_rel_bucket(delta * MB_BLOCK + t - s)] for delta in range(3)]
    return jnp.stack(tiles).transpose(0, 3, 1, 2).astype(F32)


def _moba_sample_kernel(pt_ref, q_ref, kown_ref, vown_ref, bfar_ref, blast_ref, bown_ref, *rest,
                        n_pages, pps):
    del pt_ref
    k_refs, v_refs = rest[:pps], rest[pps:2 * pps]
    o_ref = rest[2 * pps]
    s_all, p_all, msum, acc_ref, den_ref = rest[2 * pps + 1:]
    nh, hd = MB_HEADS, MB_HD
    n_tok = q_ref.shape[1]
    rows = nh * n_tok
    width = nh * hd
    ppb = MB_BLOCK // PAGE_SIZE
    n_full = n_pages // ppb
    gk = n_pages // pps
    g = pl.program_id(1)

    q = q_ref[0]
    rh = lax.broadcasted_iota(jnp.int32, (rows, width), 0) // n_tok
    ch = lax.broadcasted_iota(jnp.int32, (rows, width), 1) // hd
    own_head = rh == ch
    q_rep = jnp.concatenate([q] * nh, axis=0)
    qbd = jnp.where(own_head, q_rep * (hd ** -0.5), 0.0).astype(BF16)
    lane = lax.broadcasted_iota(jnp.int32, (rows, PAGE_SIZE), 1)

    @pl.when(g == 0)
    def _():
        msum[...] = jnp.zeros_like(msum)

    @pl.when(g < gk)
    def _():
        lane_w = lax.broadcasted_iota(jnp.int32, (width, PAGE_SIZE), 1)
        for jj in range(pps):
            kp = k_refs[jj][...].reshape(width, PAGE_SIZE)
            page = g * pps + jj
            s_all[page] = _dot(qbd, kp.astype(BF16))
            psum = jnp.sum(kp, axis=-1, keepdims=True)
            msum[...] += jnp.where(lane_w == page // ppb, psum, 0.0)

    @pl.when(g == gk - 1)
    def _():
        q_bd32 = jnp.where(own_head, q_rep, 0.0)
        gs = _dot_precise(q_bd32, msum[...] * (1.0 / MB_BLOCK))
        gs = jnp.where(lane < n_full, gs, -jnp.inf)
        sel = _top_mask(gs, lane, min(MB_TOPK, n_full))
        bfar = bfar_ref[...]

        def logits_block(n, carry):
            keep = jnp.max(jnp.where(lane == n, sel, 0.0), axis=-1, keepdims=True) > 0.5
            for pj in range(ppb):
                lg = jnp.where(keep, s_all[n * ppb + pj] + bfar, NEG)
                s_all[n * ppb + pj] = lg
                carry = jnp.maximum(carry, lg)
            return carry

        mx = lax.fori_loop(0, n_full - 1, logits_block, jnp.full((rows, PAGE_SIZE), NEG, F32))
        n_last = n_full - 1
        keep = jnp.max(jnp.where(lane == n_last, sel, 0.0), axis=-1, keepdims=True) > 0.5
        for pj in range(ppb):
            lg = jnp.where(keep, s_all[n_last * ppb + pj] + blast_ref[:, pj * PAGE_SIZE:(pj + 1) * PAGE_SIZE], NEG)
            s_all[n_last * ppb + pj] = lg
            mx = jnp.maximum(mx, lg)

        pad = jnp.zeros((PAGE_SIZE - n_tok, width), F32)
        kown = jnp.concatenate([kown_ref[0], pad], axis=0).astype(BF16)
        vown = jnp.concatenate([vown_ref[0], pad], axis=0).astype(BF16)
        tq = lax.broadcasted_iota(jnp.int32, (rows, PAGE_SIZE), 0) % n_tok
        s_own = jnp.where(lane <= tq, _dot_nt(qbd, kown) + bown_ref[...], NEG)
        mx = jnp.max(jnp.maximum(mx, s_own), axis=-1, keepdims=True)

        def probs_page(pg, carry):
            p = jnp.exp(s_all[pg] - mx)
            p_all[pg] = p.astype(BF16)
            return carry + p

        den = lax.fori_loop(0, n_pages, probs_page, jnp.zeros((rows, PAGE_SIZE), F32))
        p_own = jnp.exp(s_own - mx)
        den_ref[...] = jnp.sum(den + p_own, axis=-1, keepdims=True)
        acc_ref[...] = _dot(p_own.astype(BF16), vown)

    @pl.when(g >= gk)
    def _():
        for jj in range(pps):
            vp = v_refs[jj][...].reshape(width, PAGE_SIZE).astype(BF16)
            acc_ref[...] += _dot_nt(p_all[(g - gk) * pps + jj], vp)

    @pl.when(g == 2 * gk - 1)
    def _():
        o = jnp.where(own_head, acc_ref[...] / den_ref[...], 0.0)
        out = o[0:n_tok]
        for h in range(1, nh):
            out = out + o[h * n_tok:(h + 1) * n_tok]
        o_ref[0] = out


def _moba_sample(mq, kown, vown, ck_t, cv_t, page_table, rel_bias, layer):
    dbs, n_tok, width = mq.shape
    n_pages = page_table.shape[1]
    past = n_pages * PAGE_SIZE
    ppb = MB_BLOCK // PAGE_SIZE
    pps = PAGES_PER_STEP
    assert past % MB_BLOCK == 0 and n_pages % pps == 0 and n_pages // ppb >= MB_TOPK
    assert n_tok <= MAX_EXACT and MB_BLOCK >= MAX_DISTANCE and n_tok % 8 == 0
    rows = MB_HEADS * n_tok
    gk = n_pages // pps
    t = jnp.arange(n_tok)
    d_last = (t[:, None] + 1) + jnp.arange(MB_BLOCK)[::-1][None, :]
    blast = rel_bias[_rel_bucket(d_last)].transpose(2, 0, 1).reshape(rows, MB_BLOCK)
    bfar = jnp.broadcast_to(rel_bias[_rel_bucket(jnp.asarray(MB_BLOCK + 1))][:, None, None],
                            (MB_HEADS, n_tok, PAGE_SIZE)).reshape(rows, PAGE_SIZE)
    d_own = t[:, None] - jnp.arange(PAGE_SIZE)[None, :]
    bown = rel_bias[_rel_bucket(d_own)].transpose(2, 0, 1).reshape(rows, PAGE_SIZE)

    tok_spec = pl.BlockSpec((1, n_tok, width), lambda b, g, pt: (b, 0, 0))
    full = lambda a: pl.BlockSpec(a.shape, lambda b, g, pt: (0,) * a.ndim)
    page_block = (None, None, MB_HEADS, MB_HD, PAGE_SIZE)
    k_specs = [pl.BlockSpec(page_block, functools.partial(
        lambda b, g, pt, jj: (layer, pt[b * n_pages + jnp.minimum(g, gk - 1) * pps + jj], 0, 0, 0), jj=jj))
        for jj in range(pps)]
    v_specs = [pl.BlockSpec(page_block, functools.partial(
        lambda b, g, pt, jj: (layer, pt[b * n_pages + jnp.maximum(g - gk, 0) * pps + jj], 0, 0, 0), jj=jj))
        for jj in range(pps)]
    grid_spec = pltpu.PrefetchScalarGridSpec(
        num_scalar_prefetch=1,
        grid=(dbs, 2 * gk),
        in_specs=[tok_spec, tok_spec, tok_spec, full(bfar), full(blast), full(bown)] + k_specs + v_specs,
        out_specs=tok_spec,
        scratch_shapes=[pltpu.VMEM((n_pages, rows, PAGE_SIZE), F32),
                        pltpu.VMEM((n_pages, rows, PAGE_SIZE), BF16),
                        pltpu.VMEM((width, PAGE_SIZE), F32),
                        pltpu.VMEM((rows, width), F32),
                        pltpu.VMEM((rows, 1), F32)],
    )
    return pl.pallas_call(
        functools.partial(_moba_sample_kernel, n_pages=n_pages, pps=pps),
        grid_spec=grid_spec,
        out_shape=jax.ShapeDtypeStruct((dbs, n_tok, width), F32),
        compiler_params=pltpu.CompilerParams(dimension_semantics=("parallel", "arbitrary"),
                                             vmem_limit_bytes=VMEM_SMALL),
        name="moba_sample",
    )(page_table.reshape(-1), mq, kown, vown, bfar, blast, bown, *([ck_t] * pps), *([cv_t] * pps))


def kernel(x_prompt, x_sample, state_hgrn, cache_k, cache_v, page_table, ffn1_norm, ffn1_w_gate, ffn1_w_up,
           ffn1_w_down, mix_norm, w_in, hg_lb, hg_norm, rel_bias, w_branch_a, w_branch_b, w_out, ffn2_norm,
           ffn2_w_gate, ffn2_w_up, ffn2_w_down, final_norm):
    depth = w_in.shape[0]
    assert depth == 1, "the final norm is fused into the (single) layer's last kernel"
    bsz, seq, d = x_prompt.shape
    dbs, n_tok, _ = x_sample.shape
    nf, nv, mw = HG_HEADS * HG_DK, HG_HEADS * HG_DV, MB_HEADS * MB_HD
    n_hg = 2 * nf + 2 * nv
    assert w_in.shape[2] == n_hg + 3 * mw + 2 * d

    lbs = jnp.cumsum(jax.nn.softmax(hg_lb.astype(F32), axis=0), axis=0)
    bias_tiles = _prompt_bias_tiles(rel_bias)
    ck_t = jnp.swapaxes(cache_k, 3, 4)
    cv_t = jnp.swapaxes(cache_v, 3, 4)
    final_g = final_norm.reshape(1, d)

    xp = x_prompt
    xs = x_sample.reshape(1, dbs * n_tok, d)
    sp_l, ss_l, kp_l, vp_l, ks_l, vs_l = [], [], [], [], [], []
    yp = ys = None
    for l in range(depth):
        w = w_in[l]
        k0 = n_hg + mw
        ffn_in_w = (ffn1_norm[l].reshape(1, d), ffn1_w_gate[l].astype(BF16), ffn1_w_up[l].astype(BF16),
                    ffn1_w_down[l].astype(BF16), mix_norm[l].reshape(1, d), w[:, :k0].astype(BF16),
                    w[:, k0:k0 + mw].T.astype(BF16), w[:, k0 + mw:k0 + 2 * mw].T.astype(BF16),
                    w[:, k0 + 2 * mw:].astype(BF16))
        merge_w = (w_branch_a[l].astype(BF16), w_branch_b[l].astype(BF16), w_out[l].astype(BF16),
                   ffn2_norm[l].reshape(1, d), ffn2_w_gate[l].astype(BF16), ffn2_w_up[l].astype(BF16),
                   ffn2_w_down[l].astype(BF16), final_g)
        lb = lbs[l].reshape(1, nf)
        ng = hg_norm[l].reshape(1, HG_DV)

        x1p, zhp, mqp, ktp, vtp, gabp = _ffn_in(xp, *ffn_in_w)
        ya_p, st_p = _hgrn(zhp, jnp.zeros((bsz, HG_HEADS, HG_DK, HG_DV), F32), lb, ng, seq)
        yb_p = _moba_prompt(mqp, ktp, vtp, bias_tiles)
        yp = _merge_ffn(x1p, ya_p, yb_p, gabp, *merge_w)

        x1s, zhs, mqs, kts, vts, gabs = _ffn_in(xs, *ffn_in_w)
        pad_tok = max(16, n_tok)
        zhs_b = jnp.pad(zhs.reshape(dbs, n_tok, n_hg), ((0, 0), (0, pad_tok - n_tok), (0, 0)))
        ya_s, st_s = _hgrn(zhs_b, state_hgrn[l], lb, ng, n_tok)
        ya_s = ya_s[:, :n_tok].reshape(1, dbs * n_tok, nv)
        kown = kts[0].T.reshape(dbs, n_tok, mw)
        vown = vts[0].T.reshape(dbs, n_tok, mw)
        yb_s = _moba_sample(mqs.reshape(dbs, n_tok, mw), kown, vown, ck_t, cv_t, page_table, rel_bias, l)
        ys = _merge_ffn(x1s, ya_s, yb_s.reshape(1, dbs * n_tok, mw), gabs, *merge_w)

        sp_l.append(st_p)
        ss_l.append(st_s)
        kp_l.append(jnp.swapaxes(ktp.reshape(bsz, MB_HEADS, MB_HD, seq), 2, 3))
        vp_l.append(jnp.swapaxes(vtp.reshape(bsz, MB_HEADS, MB_HD, seq), 2, 3))
        ks_l.append(kown.reshape(dbs, n_tok, MB_HEADS, MB_HD).transpose(0, 2, 1, 3))
        vs_l.append(vown.reshape(dbs, n_tok, MB_HEADS, MB_HD).transpose(0, 2, 1, 3))
    return (yp, ys.reshape(dbs, n_tok, d), jnp.stack(sp_l), jnp.stack(ss_l), jnp.stack(kp_l), jnp.stack(vp_l),
            jnp.stack(ks_l), jnp.stack(vs_l))
```

```python
import functools
import math

import numpy as np
import jax
import jax.numpy as jnp
from jax import lax
from jax.experimental import pallas as pl
from jax.experimental.pallas import tpu as pltpu

F32 = jnp.float32
BF16 = jnp.bfloat16

EPS = 1e-6
NEG = -1e30
HG_HEADS = 4
HG_DK = 128
HG_DV = 128
MB_HEADS = 8
MB_HD = 64
MB_BLOCK = 256
MB_TOPK = 3
PAGE_SIZE = 128
N_BUCKETS = 32
MAX_EXACT = N_BUCKETS // 2
MAX_DISTANCE = 128

V7X_VMEM_BYTES = 64 * 2**20
VMEM_BIG = V7X_VMEM_BYTES - 8 * 2**20
VMEM_SMALL = 40 * 2**20

ROW_TILE = 256
HG_CHUNK = 128
HG_STEP = 512
HG_UNROLL = 2
PAGES_PER_STEP = 16

_NT = (((1,), (1,)), ((), ()))


def _dot(a, b):
    return jnp.dot(a, b, preferred_element_type=F32)


def _dot_nt(a, b):
    return lax.dot_general(a, b, _NT, preferred_element_type=F32)


def _split2(a):
    hi = a.astype(BF16)
    lo = (a - hi.astype(F32)).astype(BF16)
    return hi, lo


def _dot_precise(a, b):
    a1, a2 = _split2(a)
    b1, b2 = _split2(b)
    return _dot(a1, b1) + _dot(a1, b2) + _dot(a2, b1)


def _rms(x, g):
    return x * lax.rsqrt(jnp.mean(x * x, axis=-1, keepdims=True) + EPS) * g


def _swiglu(h, wg_ref, wu_ref, wd_ref, ff_chunk):
    d_ff = wg_ref.shape[1]
    acc = None
    for c0 in range(0, d_ff, ff_chunk):
        a = _dot(h, wg_ref[:, c0:c0 + ff_chunk])
        u = _dot(h, wu_ref[:, c0:c0 + ff_chunk])
        act = (a * jax.nn.sigmoid(a) * u).astype(BF16)
        part = _dot(act, wd_ref[c0:c0 + ff_chunk, :])
        acc = part if acc is None else acc + part
    return acc


def _ff_chunk(d_ff):
    half = d_ff // 2
    return half if (d_ff % 2 == 0 and half % 128 == 0) else d_ff


def _const_spec(shape):
    nd = len(shape)
    return pl.BlockSpec(shape, lambda *_: (0,) * nd, pipeline_mode=pl.Buffered(1))


def _ffn_in_kernel(x_ref, n1_ref, wg_ref, wu_ref, wd_ref, nm_ref, wz_ref, wkt_ref, wvt_ref, wgab_ref,
                   x1_ref, zh_ref, mq_ref, kt_ref, vt_ref, gab_ref, *, ff_chunk, n_hg):
    x = x_ref[0]
    h = _rms(x, n1_ref[...]).astype(BF16)
    x1 = x + 0.5 * _swiglu(h, wg_ref, wu_ref, wd_ref, ff_chunk)
    x1_ref[0] = x1
    h2 = _rms(x1, nm_ref[...]).astype(BF16)
    z = _dot(h2, wz_ref[...])
    zh_ref[0] = z[:, :n_hg]
    mq_ref[0] = z[:, n_hg:]
    kt_ref[0] = _dot_nt(wkt_ref[...], h2)
    vt_ref[0] = _dot_nt(wvt_ref[...], h2)
    gab_ref[0] = _dot(h2, wgab_ref[...]).astype(BF16)


def _ffn_in(x, n1, wg, wu, wd, nm, wz, wkt, wvt, wgab):
    bsz, seq, d = x.shape
    tm = min(ROW_TILE, seq)
    assert seq % tm == 0
    n_z = wz.shape[1]
    n_hg = 2 * HG_HEADS * HG_DK + 2 * HG_HEADS * HG_DV
    n_kv = wkt.shape[0]
    n_gab = wgab.shape[1]
    row = lambda w: pl.BlockSpec((1, tm, w), lambda b, i: (b, i, 0))
    col = pl.BlockSpec((1, n_kv, tm), lambda b, i: (b, 0, i))
    outs = pl.pallas_call(
        functools.partial(_ffn_in_kernel, ff_chunk=_ff_chunk(wg.shape[1]), n_hg=n_hg),
        grid=(bsz, seq // tm),
        in_specs=[row(d), _const_spec(n1.shape), _const_spec(wg.shape), _const_spec(wu.shape),
                  _const_spec(wd.shape), _const_spec(nm.shape), _const_spec(wz.shape),
                  _const_spec(wkt.shape), _const_spec(wvt.shape), _const_spec(wgab.shape)],
        out_specs=[row(d), row(n_hg), row(n_z - n_hg), col, col, row(n_gab)],
        out_shape=[jax.ShapeDtypeStruct((bsz, seq, d), F32),
                   jax.ShapeDtypeStruct((bsz, seq, n_hg), F32),
                   jax.ShapeDtypeStruct((bsz, seq, n_z - n_hg), F32),
                   jax.ShapeDtypeStruct((bsz, n_kv, seq), F32),
                   jax.ShapeDtypeStruct((bsz, n_kv, seq), F32),
                   jax.ShapeDtypeStruct((bsz, seq, n_gab), BF16)],
        compiler_params=pltpu.CompilerParams(dimension_semantics=("parallel", "parallel"),
                                             vmem_limit_bytes=VMEM_BIG),
        name="ffn_in",
    )(x, n1, wg, wu, wd, nm, wz, wkt, wvt, wgab)
    return outs


def _merge_ffn_kernel(x1_ref, ya_ref, yb_ref, gab_ref, wa_ref, wb_ref, wo_ref, n2_ref,
                      wg_ref, wu_ref, wd_ref, nf_ref, y_ref, *, ff_chunk):
    d = x1_ref.shape[-1]
    gab = gab_ref[0].astype(F32)
    ma = _dot(ya_ref[0].astype(BF16), wa_ref[...])
    mb = _dot(yb_ref[0].astype(BF16), wb_ref[...])
    m = jax.nn.sigmoid(gab[:, :d]) * ma + jax.nn.sigmoid(gab[:, d:]) * mb
    x2 = x1_ref[0] + _dot(m.astype(BF16), wo_ref[...])
    h = _rms(x2, n2_ref[...]).astype(BF16)
    x3 = x2 + 0.5 * _swiglu(h, wg_ref, wu_ref, wd_ref, ff_chunk)
    y_ref[0] = _rms(x3, nf_ref[...])


def _merge_ffn(x1, ya, yb, gab, wa, wb, wo, n2, wg, wu, wd, nf):
    bsz, seq, d = x1.shape
    tm = min(ROW_TILE, seq)
    assert seq % tm == 0
    row = lambda w: pl.BlockSpec((1, tm, w), lambda b, i: (b, i, 0))
    return pl.pallas_call(
        functools.partial(_merge_ffn_kernel, ff_chunk=_ff_chunk(wg.shape[1])),
        grid=(bsz, seq // tm),
        in_specs=[row(d), row(ya.shape[-1]), row(yb.shape[-1]), row(gab.shape[-1]),
                  _const_spec(wa.shape), _const_spec(wb.shape), _const_spec(wo.shape), _const_spec(n2.shape),
                  _const_spec(wg.shape), _const_spec(wu.shape), _const_spec(wd.shape), _const_spec(nf.shape)],
        out_specs=row(d),
        out_shape=jax.ShapeDtypeStruct((bsz, seq, d), F32),
        compiler_params=pltpu.CompilerParams(dimension_semantics=("parallel", "parallel"),
                                             vmem_limit_bytes=VMEM_BIG),
        name="merge_ffn",
    )(x1, ya, yb, gab, wa, wb, wo, n2, wg, wu, wd, nf)


def _hgrn_levels(chunk):
    levels, m = [], 1
    while m < chunk:
        levels.append(m)
        m *= 2
    return levels


HG_SPLIT = 2


def _hgrn_decay_matrix(chunk):
    t = np.arange(chunk)[:, None]
    u = np.arange(chunk)[None, :]
    mats = [u <= t]
    for m in _hgrn_levels(chunk):
        mid_q = t & ~(m - 1)
        q_side = ((t & m) != 0) & (u >= mid_q) & (u <= t)
        mid_k = (t & ~(2 * m - 1)) + m
        k_side = ((t & m) == 0) & (u > t) & (u < mid_k)
        mats.append(q_side | k_side)
    return np.tile(np.concatenate(mats, axis=0).astype(np.float32), (1, HG_SPLIT))


def _hgrn_kernel(zh_ref, s0_ref, lb_ref, ng_ref, dm_ref, ya_ref, sout_ref, st_ref,
                 *, chunk, n_inner, valid_len):
    nh, dk, dv = HG_HEADS, HG_DK, HG_DV
    nf = nh * dk
    j = pl.program_id(1)

    @pl.when(j == 0)
    def _():
        for h in range(nh):
            st_ref[h] = s0_ref[0, h].T

    lb = lb_ref[...]
    ng = ng_ref[...]
    dmat = dm_ref[...]
    ti = lax.broadcasted_iota(jnp.int32, (chunk, chunk), 0)
    si = lax.broadcasted_iota(jnp.int32, (chunk, chunk), 1)
    xs = jnp.where(ti > si, jnp.bitwise_xor(ti, si), 0)
    levels = _hgrn_levels(chunk)

    def one_chunk(ci, carry):
        r0 = pl.multiple_of(ci * chunk, chunk)
        zh = zh_ref[0, pl.ds(r0, chunk), :]
        hq, hf = zh[:, :nf], zh[:, nf:2 * nf]
        hv, hg = zh[:, 2 * nf:2 * nf + nh * dv], zh[:, 2 * nf + nh * dv:]
        logf = jnp.log(lb + (1.0 - lb) * jax.nn.sigmoid(hf))
        kk = (1.0 - lb) * jax.nn.sigmoid(-hf)
        if valid_len < chunk:
            live = lax.broadcasted_iota(jnp.int32, (chunk, 1), 0) < valid_len
            logf = jnp.where(live, logf, 0.0)
            kk = jnp.where(live, kk, 0.0)
        qf = hq * jax.nn.sigmoid(hq)
        pieces, rest = [], logf
        for _ in range(HG_SPLIT):
            piece = rest.astype(BF16)
            pieces.append(piece)
            rest = rest - piece.astype(F32)
        dec = _dot(dmat, jnp.concatenate(pieces, axis=0))
        outs = []
        for h in range(nh):
            ks = slice(h * dk, (h + 1) * dk)
            vs = slice(h * dv, (h + 1) * dv)
            q_h, k_h, v_h = qf[:, ks], kk[:, ks], hv[:, vs]
            b = dec[0:chunk, ks]
            after = b[chunk - 1:chunk, :] - b
            st = st_ref[h]
            o = _dot_nt((q_h * jnp.exp(b)).astype(BF16), st.astype(BF16))
            att = None
            for li, m in enumerate(levels):
                w = jnp.exp(dec[(1 + li) * chunk:(2 + li) * chunk, ks])
                a_l = _dot_nt((q_h * w).astype(BF16), (k_h * w).astype(BF16))
                att = a_l if att is None else jnp.where(xs >= m, a_l, att)
            diag = jnp.sum(q_h * k_h, axis=-1, keepdims=True)
            att = jnp.where(xs >= 1, att, 0.0)
            att = jnp.where(ti == si, diag, att)
            o = o + _dot(att.astype(BF16), v_h.astype(BF16))
            kd = (k_h * jnp.exp(after)).astype(BF16)
            st_ref[h] = jnp.exp(b[chunk - 1:chunk, :]) * st + _dot(v_h.T.astype(BF16), kd)
            o = o * lax.rsqrt(jnp.mean(o * o, axis=-1, keepdims=True) + EPS) * ng
            g_h = hg[:, vs]
            outs.append((o * (g_h * jax.nn.sigmoid(g_h))).astype(BF16))
        ya_ref[0, pl.ds(r0, chunk), :] = jnp.concatenate(outs, axis=-1)
        return carry

    lax.fori_loop(0, n_inner, one_chunk, 0, unroll=min(HG_UNROLL, n_inner))

    @pl.when(j == pl.num_programs(1) - 1)
    def _():
        for h in range(nh):
            sout_ref[0, h] = st_ref[h].T


def _hgrn(zh, s0, lb, ng, valid_len):
    bsz, seq, width = zh.shape
    chunk = min(HG_CHUNK, seq)
    step = min(HG_STEP, seq)
    assert seq % step == 0 and step % chunk == 0 and chunk % 16 == 0
    dmat = jnp.asarray(_hgrn_decay_matrix(chunk), BF16)
    nv = HG_HEADS * HG_DV
    st_spec = pl.BlockSpec((1, HG_HEADS, HG_DK, HG_DV), lambda b, j: (b, 0, 0, 0))
    return pl.pallas_call(
        functools.partial(_hgrn_kernel, chunk=chunk, n_inner=step // chunk, valid_len=min(valid_len, chunk)),
        grid=(bsz, seq // step),
        in_specs=[pl.BlockSpec((1, step, width), lambda b, j: (b, j, 0)), st_spec,
                  _const_spec(lb.shape), _const_spec(ng.shape), _const_spec(dmat.shape)],
        out_specs=[pl.BlockSpec((1, step, nv), lambda b, j: (b, j, 0)), st_spec],
        out_shape=[jax.ShapeDtypeStruct((bsz, seq, nv), BF16),
                   jax.ShapeDtypeStruct((bsz, HG_HEADS, HG_DK, HG_DV), F32)],
        scratch_shapes=[pltpu.VMEM((HG_HEADS, HG_DV, HG_DK), F32)],
        compiler_params=pltpu.CompilerParams(dimension_semantics=("parallel", "arbitrary"),
                                             vmem_limit_bytes=VMEM_SMALL),
        name="hgrn",
    )(zh, s0, lb, ng, dmat)


def _rel_bucket(dist):
    n = jnp.maximum(dist, 0)
    nf = jnp.maximum(n, 1).astype(F32)
    large = MAX_EXACT + (jnp.log(nf / MAX_EXACT) / math.log(MAX_DISTANCE / MAX_EXACT)
                         * (N_BUCKETS - MAX_EXACT)).astype(jnp.int32)
    large = jnp.minimum(large, N_BUCKETS - 1)
    return jnp.where(n < MAX_EXACT, n, large)


def _top_mask(gs, lane, n_sel):
    mask = jnp.zeros(gs.shape, F32)
    lane_f = lane.astype(F32)
    cur = gs
    for _ in range(n_sel):
        best = jnp.max(cur, axis=-1, keepdims=True)
        idx = jnp.min(jnp.where(cur == best, lane_f, float(gs.shape[-1])), axis=-1, keepdims=True)
        hit = lane_f == idx
        mask = jnp.where(hit, 1.0, mask)
        cur = jnp.where(hit, -jnp.inf, cur)
    return mask


def _moba_prompt_kernel(th_ref, rb_ref, q_ref, kt_ref, vt_ref, o_ref, bias_sc, *, n_blocks):
    blk, hd = MB_BLOCK, MB_HD
    heads = q_ref.shape[-1] // hd
    seq = q_ref.shape[1]
    log2e = 1.0 / math.log(2.0)
    scale = hd ** -0.5 * log2e
    grp = pl.program_id(0)
    lane = lax.broadcasted_iota(jnp.int32, (blk, 128), 1)
    ti = lax.broadcasted_iota(jnp.int32, (blk, blk), 0)
    si = lax.broadcasted_iota(jnp.int32, (blk, blk), 1)
    causal = ti >= si

    @pl.when(pl.program_id(1) == 0)
    def _():
        for delta in range(2):
            d = jnp.maximum(delta * blk + ti - si, 0)
            for hl in range(heads):
                base = (grp * heads + hl) * N_BUCKETS
                far = rb_ref[base + N_BUCKETS - 1]
                val = jnp.full((blk, blk), far, F32)
                for bk in range(N_BUCKETS - 2, -1, -1):
                    val = jnp.where(d < th_ref[bk + 1], rb_ref[base + bk], val)
                bias_sc[delta, hl] = (val - far) * log2e

    row = lax.broadcasted_iota(jnp.int32, (hd, seq), 0)
    col_blk = lax.broadcasted_iota(jnp.int32, (hd, seq), 1) // blk
    indicator = jnp.where(row == col_blk, 1.0, 0.0).astype(BF16)
    ones_row = jnp.where(row == 0, 1.0, 0.0).astype(BF16)

    q_all = q_ref[0]
    qs, ktx, vtx, gss = [], [], [], []
    for hl in range(heads):
        q = q_all[:, hl * hd:(hl + 1) * hd]
        kt = kt_ref[0, hl * hd:(hl + 1) * hd, :]
        qs.append((q * scale).astype(BF16))
        ktx.append(jnp.concatenate([kt.astype(BF16), indicator], axis=0))
        vtx.append(jnp.concatenate([vt_ref[0, hl * hd:(hl + 1) * hd, :].astype(BF16), ones_row], axis=0))
        if n_blocks > 1:
            lane_k = lax.broadcasted_iota(jnp.int32, (hd, 128), 1)
            kmean = jnp.zeros((hd, 128), F32)
            for n in range(n_blocks):
                ksum = jnp.sum(kt[:, n * blk:(n + 1) * blk], axis=-1, keepdims=True)
                kmean = jnp.where(lane_k == n, ksum * (1.0 / blk), kmean)
            gss.append(_dot_precise(q, kmean))

    for i in range(n_blocks):
        rows = slice(i * blk, (i + 1) * blk)
        outs = []
        for hl in range(heads):
            n_sel = min(MB_TOPK, i)
            if n_sel > 0:
                gs = jnp.where(lane < i, gss[hl][rows], -jnp.inf)
                sel = _top_mask(gs, lane, n_sel)
                pen = jnp.where((sel > 0.5) | (lane >= i), 0.0, NEG)[:, :hd].astype(BF16)
            else:
                pen = jnp.zeros((blk, hd), BF16)
            qx = jnp.concatenate([qs[hl][rows], pen], axis=-1)
            s_all = _dot(qx, ktx[hl][:, :(i + 1) * blk])
            tiles = []
            for jb in range(i + 1):
                s = s_all[:, jb * blk:(jb + 1) * blk]
                if jb == i:
                    s = jnp.where(causal, s + bias_sc[0, hl], NEG)
                elif jb == i - 1:
                    s = s + bias_sc[1, hl]
                tiles.append(s)
            mx = tiles[0]
            for s in tiles[1:]:
                mx = jnp.maximum(mx, s)
            mx = jnp.max(mx, axis=-1, keepdims=True)
            p = jnp.concatenate([jnp.exp2(s - mx).astype(BF16) for s in tiles], axis=-1)
            acc = _dot_nt(p, vtx[hl][:, :(i + 1) * blk])
            outs.append((acc[:, :hd] / acc[:, hd:hd + 1]).astype(BF16))
        o_ref[0, rows, :] = jnp.concatenate(outs, axis=-1)


def _bucket_thresholds():
    buckets = _rel_bucket(jnp.arange(MAX_DISTANCE + 1))
    return jnp.sum(buckets[None, :] < jnp.arange(N_BUCKETS + 1)[:, None], axis=1).astype(jnp.int32)


def _moba_prompt(mq, kt, vt, rel_bias):
    bsz, seq, width = mq.shape
    assert seq % MB_BLOCK == 0 and MB_BLOCK >= MAX_DISTANCE and seq // MB_BLOCK <= MB_HD
    hp = 128 // MB_HD
    n_groups = width // 128
    smem = pl.BlockSpec(memory_space=pltpu.SMEM)
    return pl.pallas_call(
        functools.partial(_moba_prompt_kernel, n_blocks=seq // MB_BLOCK),
        grid=(n_groups, bsz),
        in_specs=[smem, smem,
                  pl.BlockSpec((1, seq, 128), lambda g, b: (b, 0, g)),
                  pl.BlockSpec((1, 128, seq), lambda g, b: (b, g, 0)),
                  pl.BlockSpec((1, 128, seq), lambda g, b: (b, g, 0))],
        out_specs=pl.BlockSpec((1, seq, 128), lambda g, b: (b, 0, g)),
        out_shape=jax.ShapeDtypeStruct((bsz, seq, width), BF16),
        scratch_shapes=[pltpu.VMEM((2, hp, MB_BLOCK, MB_BLOCK), F32)],
        compiler_params=pltpu.CompilerParams(dimension_semantics=("parallel", "arbitrary"),
                                             vmem_limit_bytes=VMEM_SMALL),
        name="moba_prompt",
    )(_bucket_thresholds(), rel_bias.T.reshape(-1).astype(F32), mq, kt, vt)


def _moba_sample_kernel(pt_ref, q_ref, kown_ref, vown_ref, bfar_ref, blast_ref, bown_ref, *rest,
                        n_pages, pps):
    del pt_ref
    k_refs, v_refs = rest[:pps], rest[pps:2 * pps]
    o_ref = rest[2 * pps]
    s_all, msum, pmax, shift_sc, acc_ref, den_ref = rest[2 * pps + 1:]
    nh, hd = MB_HEADS, MB_HD
    n_tok = q_ref.shape[1]
    rows = nh * n_tok
    width = nh * hd
    ppb = MB_BLOCK // PAGE_SIZE
    n_full = n_pages // ppb
    gk = n_pages // pps
    g = pl.program_id(1)

    q = q_ref[0]
    rh = lax.broadcasted_iota(jnp.int32, (rows, width), 0) // n_tok
    ch = lax.broadcasted_iota(jnp.int32, (rows, width), 1) // hd
    own_head = rh == ch
    q_rep = jnp.concatenate([q] * nh, axis=0)
    qbd = jnp.where(own_head, q_rep * (hd ** -0.5), 0.0).astype(BF16)
    lane = lax.broadcasted_iota(jnp.int32, (rows, PAGE_SIZE), 1)

    @pl.when(g == 0)
    def _():
        msum[...] = jnp.zeros_like(msum)
        pmax[...] = jnp.full_like(pmax, -jnp.inf)

    @pl.when(g < gk)
    def _():
        lane_w = lax.broadcasted_iota(jnp.int32, (width, PAGE_SIZE), 1)
        pm = pmax[...]
        ms = msum[...]
        for jb in range(pps // ppb):
            ksum = None
            for pj in range(ppb):
                jj = jb * ppb + pj
                kp = k_refs[jj][...].reshape(width, PAGE_SIZE)
                page = g * pps + jj
                s = _dot(qbd, kp.astype(BF16))
                s_all[page] = s
                pm = jnp.where(lane == page, jnp.max(s, axis=-1, keepdims=True), pm)
                ksum = kp if ksum is None else ksum + kp
            ms = jnp.where(lane_w == g * (pps // ppb) + jb, jnp.sum(ksum, axis=-1, keepdims=True), ms)
        pmax[...] = pm
        msum[...] = ms

    @pl.when(g == gk - 1)
    def _():
        q_bd32 = jnp.where(own_head, q_rep, 0.0)
        gs = _dot_precise(q_bd32, msum[...] * (1.0 / MB_BLOCK))
        gs = jnp.where(lane < n_full, gs, -jnp.inf)
        sel = _top_mask(gs, lane, min(MB_TOPK, n_full))
        blk_of_page = lax.broadcasted_iota(jnp.int32, (PAGE_SIZE, PAGE_SIZE), 1) // ppb
        blk_row = lax.broadcasted_iota(jnp.int32, (PAGE_SIZE, PAGE_SIZE), 0)
        sel_page = _dot(sel.astype(BF16), jnp.where(blk_row == blk_of_page, 1.0, 0.0).astype(BF16)) > 0.5
        bfar = bfar_ref[...]
        pm = pmax[...]
        for pj in range(ppb):
            page = n_pages - ppb + pj
            s = s_all[page] + (blast_ref[:, pj * PAGE_SIZE:(pj + 1) * PAGE_SIZE] - bfar)
            s_all[page] = s
            pm = jnp.where(lane == page, jnp.max(s, axis=-1, keepdims=True), pm)

        pad = jnp.zeros((PAGE_SIZE - n_tok, width), F32)
        kown = jnp.concatenate([kown_ref[0], pad], axis=0).astype(BF16)
        vown = jnp.concatenate([vown_ref[0], pad], axis=0).astype(BF16)
        tq = lax.broadcasted_iota(jnp.int32, (rows, PAGE_SIZE), 0) % n_tok
        s_own = jnp.where(lane <= tq, _dot_nt(qbd, kown) + (bown_ref[...] - bfar), -jnp.inf)
        mx = jnp.max(jnp.maximum(jnp.where(sel_page, pm, -jnp.inf), s_own), axis=-1, keepdims=True)
        shift_sc[...] = jnp.where(sel_page, mx, jnp.inf)
        p_own = jnp.exp(s_own - mx)
        den_ref[...] = p_own
        acc_ref[...] = _dot(p_own.astype(BF16), vown)

    @pl.when(g >= gk)
    def _():
        shift = shift_sc[...]
        den = den_ref[...]
        acc = acc_ref[...]
        for jj in range(pps):
            page = (g - gk) * pps + jj
            sh = jnp.min(jnp.where(lane == page, shift, jnp.inf), axis=-1, keepdims=True)
            p = jnp.exp(s_all[page] - sh)
            den = den + p
            acc = acc + _dot_nt(p.astype(BF16), v_refs[jj][...].reshape(width, PAGE_SIZE).astype(BF16))
        den_ref[...] = den
        acc_ref[...] = acc

    @pl.when(g == 2 * gk - 1)
    def _():
        o = jnp.where(own_head, acc_ref[...] / jnp.sum(den_ref[...], axis=-1, keepdims=True), 0.0)
        out = o[0:n_tok]
        for h in range(1, nh):
            out = out + o[h * n_tok:(h + 1) * n_tok]
        o_ref[0] = out


def _moba_sample(mq, kown, vown, ck_t, cv_t, page_table, rel_bias, layer):
    dbs, n_tok, width = mq.shape
    n_pages = page_table.shape[1]
    past = n_pages * PAGE_SIZE
    ppb = MB_BLOCK // PAGE_SIZE
    pps = PAGES_PER_STEP
    assert past % MB_BLOCK == 0 and n_pages % pps == 0 and pps % ppb == 0 and n_pages // ppb >= MB_TOPK
    assert n_tok <= MAX_EXACT and MB_BLOCK >= MAX_DISTANCE and n_tok % 8 == 0 and n_pages <= PAGE_SIZE
    rows = MB_HEADS * n_tok
    gk = n_pages // pps
    t = jnp.arange(n_tok)
    d_last = (t[:, None] + 1) + jnp.arange(MB_BLOCK)[::-1][None, :]
    blast = rel_bias[_rel_bucket(d_last)].transpose(2, 0, 1).reshape(rows, MB_BLOCK)
    bfar = jnp.broadcast_to(rel_bias[_rel_bucket(jnp.asarray(MB_BLOCK + 1))][:, None, None],
                            (MB_HEADS, n_tok, PAGE_SIZE)).reshape(rows, PAGE_SIZE)
    d_own = t[:, None] - jnp.arange(PAGE_SIZE)[None, :]
    bown = rel_bias[_rel_bucket(d_own)].transpose(2, 0, 1).reshape(rows, PAGE_SIZE)

    tok_spec = pl.BlockSpec((1, n_tok, width), lambda b, g, pt: (b, 0, 0))
    full = lambda a: pl.BlockSpec(a.shape, lambda b, g, pt: (0,) * a.ndim)
    page_block = (None, None, MB_HEADS, MB_HD, PAGE_SIZE)
    k_specs = [pl.BlockSpec(page_block, functools.partial(
        lambda b, g, pt, jj: (layer, pt[b * n_pages + jnp.minimum(g, gk - 1) * pps + jj], 0, 0, 0), jj=jj))
        for jj in range(pps)]
    v_specs = [pl.BlockSpec(page_block, functools.partial(
        lambda b, g, pt, jj: (layer, pt[b * n_pages + jnp.maximum(g - gk, 0) * pps + jj], 0, 0, 0), jj=jj))
        for jj in range(pps)]
    grid_spec = pltpu.PrefetchScalarGridSpec(
        num_scalar_prefetch=1,
        grid=(dbs, 2 * gk),
        in_specs=[tok_spec, tok_spec, tok_spec, full(bfar), full(blast), full(bown)] + k_specs + v_specs,
        out_specs=tok_spec,
        scratch_shapes=[pltpu.VMEM((n_pages, rows, PAGE_SIZE), F32),
                        pltpu.VMEM((width, PAGE_SIZE), F32),
                        pltpu.VMEM((rows, PAGE_SIZE), F32),
                        pltpu.VMEM((rows, PAGE_SIZE), F32),
                        pltpu.VMEM((rows, width), F32),
                        pltpu.VMEM((rows, PAGE_SIZE), F32)],
    )
    return pl.pallas_call(
        functools.partial(_moba_sample_kernel, n_pages=n_pages, pps=pps),
        grid_spec=grid_spec,
        out_shape=jax.ShapeDtypeStruct((dbs, n_tok, width), F32),
        compiler_params=pltpu.CompilerParams(dimension_semantics=("parallel", "arbitrary"),
                                             vmem_limit_bytes=VMEM_SMALL),
        name="moba_sample",
    )(page_table.reshape(-1), mq, kown, vown, bfar, blast, bown, *([ck_t] * pps), *([cv_t] * pps))


def kernel(x_prompt, x_sample, state_hgrn, cache_k, cache_v, page_table, ffn1_norm, ffn1_w_gate, ffn1_w_up,
           ffn1_w_down, mix_norm, w_in, hg_lb, hg_norm, rel_bias, w_branch_a, w_branch_b, w_out, ffn2_norm,
           ffn2_w_gate, ffn2_w_up, ffn2_w_down, final_norm):
    depth = w_in.shape[0]
    assert depth == 1, "the final norm is fused into the (single) layer's last kernel"
    bsz, seq, d = x_prompt.shape
    dbs, n_tok, _ = x_sample.shape
    nf, nv, mw = HG_HEADS * HG_DK, HG_HEADS * HG_DV, MB_HEADS * MB_HD
    n_hg = 2 * nf + 2 * nv
    assert w_in.shape[2] == n_hg + 3 * mw + 2 * d

    lbs = jnp.cumsum(jax.nn.softmax(hg_lb.astype(F32), axis=0), axis=0)
    ck_t = jnp.swapaxes(cache_k, 3, 4)
    cv_t = jnp.swapaxes(cache_v, 3, 4)
    final_g = final_norm.reshape(1, d)

    xp = x_prompt
    xs = x_sample.reshape(1, dbs * n_tok, d)
    sp_l, ss_l, kp_l, vp_l, ks_l, vs_l = [], [], [], [], [], []
    yp = ys = None
    for l in range(depth):
        w = w_in[l]
        k0 = n_hg + mw
        ffn_in_w = (ffn1_norm[l].reshape(1, d), ffn1_w_gate[l].astype(BF16), ffn1_w_up[l].astype(BF16),
                    ffn1_w_down[l].astype(BF16), mix_norm[l].reshape(1, d), w[:, :k0].astype(BF16),
                    w[:, k0:k0 + mw].T.astype(BF16), w[:, k0 + mw:k0 + 2 * mw].T.astype(BF16),
                    w[:, k0 + 2 * mw:].astype(BF16))
        merge_w = (w_branch_a[l].astype(BF16), w_branch_b[l].astype(BF16), w_out[l].astype(BF16),
                   ffn2_norm[l].reshape(1, d), ffn2_w_gate[l].astype(BF16), ffn2_w_up[l].astype(BF16),
                   ffn2_w_down[l].astype(BF16), final_g)
        lb = lbs[l].reshape(1, nf)
        ng = hg_norm[l].reshape(1, HG_DV)

        x1p, zhp, mqp, ktp, vtp, gabp = _ffn_in(xp, *ffn_in_w)
        ya_p, st_p = _hgrn(zhp, jnp.zeros((bsz, HG_HEADS, HG_DK, HG_DV), F32), lb, ng, seq)
        yb_p = _moba_prompt(mqp, ktp, vtp, rel_bias)
        yp = _merge_ffn(x1p, ya_p, yb_p, gabp, *merge_w)

        x1s, zhs, mqs, kts, vts, gabs = _ffn_in(xs, *ffn_in_w)
        pad_tok = max(16, n_tok)
        zhs_b = jnp.pad(zhs.reshape(dbs, n_tok, n_hg), ((0, 0), (0, pad_tok - n_tok), (0, 0)))
        ya_s, st_s = _hgrn(zhs_b, state_hgrn[l], lb, ng, n_tok)
        ya_s = ya_s[:, :n_tok].reshape(1, dbs * n_tok, nv)
        kown = kts[0].T.reshape(dbs, n_tok, mw)
        vown = vts[0].T.reshape(dbs, n_tok, mw)
        yb_s = _moba_sample(mqs.reshape(dbs, n_tok, mw), kown, vown, ck_t, cv_t, page_table, rel_bias, l)
        ys = _merge_ffn(x1s, ya_s, yb_s.reshape(1, dbs * n_tok, mw), gabs, *merge_w)

        sp_l.append(st_p)
        ss_l.append(st_s)
        kp_l.append(jnp.swapaxes(ktp.reshape(bsz, MB_HEADS, MB_HD, seq), 2, 3))
        vp_l.append(jnp.swapaxes(vtp.reshape(bsz, MB_HEADS, MB_HD, seq), 2, 3))
        ks_l.append(kown.reshape(dbs, n_tok, MB_HEADS, MB_HD).transpose(0, 2, 1, 3))
        vs_l.append(vown.reshape(dbs, n_tok, MB_HEADS, MB_HD).transpose(0, 2, 1, 3))
    return (yp, ys.reshape(dbs, n_tok, d), jnp.stack(sp_l), jnp.stack(ss_l), jnp.stack(kp_l), jnp.stack(vp_l),
            jnp.stack(ks_l), jnp.stack(vs_l))
```

```python
import functools
import math

import numpy as np
import jax
import jax.numpy as jnp
from jax import lax
from jax.experimental import pallas as pl
from jax.experimental.pallas import tpu as pltpu

F32 = jnp.float32
BF16 = jnp.bfloat16

EPS = 1e-6
NEG = -1e30
HG_HEADS = 4
HG_DK = 128
HG_DV = 128
MB_HEADS = 8
MB_HD = 64
MB_BLOCK = 256
MB_TOPK = 3
PAGE_SIZE = 128
N_BUCKETS = 32
MAX_EXACT = N_BUCKETS // 2
MAX_DISTANCE = 128

V7X_VMEM_BYTES = 64 * 2**20
MXU_DEPTH = 256
VMEM_BIG = V7X_VMEM_BYTES - 8 * 2**20
VMEM_SMALL = 40 * 2**20

ROW_TILE = 256
HG_CHUNK = 128
HG_STEP = 512
HG_UNROLL = 2
HG_SHORT_SEQS = 8
PAGES_PER_STEP = 8
RING_SLOTS = 4

_NT = (((1,), (1,)), ((), ()))


def _dot(a, b):
    return jnp.dot(a, b, preferred_element_type=F32)


def _dot_nt(a, b):
    return lax.dot_general(a, b, _NT, preferred_element_type=F32)


def _split2(a):
    hi = a.astype(BF16)
    lo = (a - hi.astype(F32)).astype(BF16)
    return hi, lo


def _dot_precise(a, b):
    a1, a2 = _split2(a)
    b1, b2 = _split2(b)
    if 3 * a.shape[-1] <= MXU_DEPTH:
        return _dot(jnp.concatenate([a1, a1, a2], axis=-1), jnp.concatenate([b1, b2, b1], axis=0))
    return _dot(a1, b1) + _dot(a1, b2) + _dot(a2, b1)


def _rms(x, g):
    return x * lax.rsqrt(jnp.mean(x * x, axis=-1, keepdims=True) + EPS) * g


def _swiglu(h, wg_ref, wu_ref, wd_ref, ff_chunk):
    d_ff = wg_ref.shape[1]
    acc = None
    for c0 in range(0, d_ff, ff_chunk):
        a = _dot(h, wg_ref[:, c0:c0 + ff_chunk])
        u = _dot(h, wu_ref[:, c0:c0 + ff_chunk])
        act = (a * jax.nn.sigmoid(a) * u).astype(BF16)
        part = _dot(act, wd_ref[c0:c0 + ff_chunk, :])
        acc = part if acc is None else acc + part
    return acc


def _ff_chunk(d_ff):
    half = d_ff // 2
    return half if (d_ff % 2 == 0 and half % 128 == 0) else d_ff


def _const_spec(shape):
    nd = len(shape)
    return pl.BlockSpec(shape, lambda *_: (0,) * nd, pipeline_mode=pl.Buffered(1))


def _ffn_in_kernel(x_ref, n1_ref, wg_ref, wu_ref, wd_ref, nm_ref, wz_ref, wkt_ref, wvt_ref, wgab_ref,
                   x1_ref, zh_ref, mq_ref, kt_ref, vt_ref, gab_ref, *, ff_chunk, n_hg):
    x = x_ref[0]
    h = _rms(x, n1_ref[...]).astype(BF16)
    x1 = x + 0.5 * _swiglu(h, wg_ref, wu_ref, wd_ref, ff_chunk)
    x1_ref[0] = x1
    h2 = _rms(x1, nm_ref[...]).astype(BF16)
    z = _dot(h2, wz_ref[...])
    zh_ref[0] = z[:, :n_hg]
    mq_ref[0] = z[:, n_hg:]
    kt_ref[0] = _dot_nt(wkt_ref[...], h2)
    vt_ref[0] = _dot_nt(wvt_ref[...], h2)
    gab_ref[0] = _dot(h2, wgab_ref[...]).astype(BF16)


def _ffn_in(x, n1, wg, wu, wd, nm, wz, wkt, wvt, wgab):
    bsz, seq, d = x.shape
    tm = min(ROW_TILE, seq)
    assert seq % tm == 0
    n_z = wz.shape[1]
    n_hg = 2 * HG_HEADS * HG_DK + 2 * HG_HEADS * HG_DV
    n_kv = wkt.shape[0]
    n_gab = wgab.shape[1]
    row = lambda w: pl.BlockSpec((1, tm, w), lambda b, i: (b, i, 0))
    col = pl.BlockSpec((1, n_kv, tm), lambda b, i: (b, 0, i))
    outs = pl.pallas_call(
        functools.partial(_ffn_in_kernel, ff_chunk=_ff_chunk(wg.shape[1]), n_hg=n_hg),
        grid=(bsz, seq // tm),
        in_specs=[row(d), _const_spec(n1.shape), _const_spec(wg.shape), _const_spec(wu.shape),
                  _const_spec(wd.shape), _const_spec(nm.shape), _const_spec(wz.shape),
                  _const_spec(wkt.shape), _const_spec(wvt.shape), _const_spec(wgab.shape)],
        out_specs=[row(d), row(n_hg), row(n_z - n_hg), col, col, row(n_gab)],
        out_shape=[jax.ShapeDtypeStruct((bsz, seq, d), F32),
                   jax.ShapeDtypeStruct((bsz, seq, n_hg), F32),
                   jax.ShapeDtypeStruct((bsz, seq, n_z - n_hg), F32),
                   jax.ShapeDtypeStruct((bsz, n_kv, seq), F32),
                   jax.ShapeDtypeStruct((bsz, n_kv, seq), F32),
                   jax.ShapeDtypeStruct((bsz, seq, n_gab), BF16)],
        compiler_params=pltpu.CompilerParams(dimension_semantics=("parallel", "parallel"),
                                             vmem_limit_bytes=VMEM_BIG),
        name="ffn_in",
    )(x, n1, wg, wu, wd, nm, wz, wkt, wvt, wgab)
    return outs


def _merge_ffn_kernel(x1_ref, ya_ref, yb_ref, gab_ref, wa_ref, wb_ref, wo_ref, n2_ref,
                      wg_ref, wu_ref, wd_ref, nf_ref, y_ref, *, ff_chunk):
    d = x1_ref.shape[-1]
    gab = gab_ref[0].astype(F32)
    ma = _dot(ya_ref[0].astype(BF16), wa_ref[...])
    mb = _dot(yb_ref[0].astype(BF16), wb_ref[...])
    m = jax.nn.sigmoid(gab[:, :d]) * ma + jax.nn.sigmoid(gab[:, d:]) * mb
    x2 = x1_ref[0] + _dot(m.astype(BF16), wo_ref[...])
    h = _rms(x2, n2_ref[...]).astype(BF16)
    x3 = x2 + 0.5 * _swiglu(h, wg_ref, wu_ref, wd_ref, ff_chunk)
    y_ref[0] = _rms(x3, nf_ref[...])


def _merge_ffn(x1, ya, yb, gab, wa, wb, wo, n2, wg, wu, wd, nf):
    bsz, seq, d = x1.shape
    tm = min(ROW_TILE, seq)
    assert seq % tm == 0
    row = lambda w: pl.BlockSpec((1, tm, w), lambda b, i: (b, i, 0))
    return pl.pallas_call(
        functools.partial(_merge_ffn_kernel, ff_chunk=_ff_chunk(wg.shape[1])),
        grid=(bsz, seq // tm),
        in_specs=[row(d), row(ya.shape[-1]), row(yb.shape[-1]), row(gab.shape[-1]),
                  _const_spec(wa.shape), _const_spec(wb.shape), _const_spec(wo.shape), _const_spec(n2.shape),
                  _const_spec(wg.shape), _const_spec(wu.shape), _const_spec(wd.shape), _const_spec(nf.shape)],
        out_specs=row(d),
        out_shape=jax.ShapeDtypeStruct((bsz, seq, d), F32),
        compiler_params=pltpu.CompilerParams(dimension_semantics=("parallel", "parallel"),
                                             vmem_limit_bytes=VMEM_BIG),
        name="merge_ffn",
    )(x1, ya, yb, gab, wa, wb, wo, n2, wg, wu, wd, nf)


def _hgrn_levels(chunk):
    levels, m = [], 1
    while m < chunk:
        levels.append(m)
        m *= 2
    return levels


HG_SPLIT = 2


def _hgrn_decay_matrix(chunk):
    t = np.arange(chunk)[:, None]
    u = np.arange(chunk)[None, :]
    mats = [u <= t]
    for m in _hgrn_levels(chunk):
        mid_q = t & ~(m - 1)
        q_side = ((t & m) != 0) & (u >= mid_q) & (u <= t)
        mid_k = (t & ~(2 * m - 1)) + m
        k_side = ((t & m) == 0) & (u > t) & (u < mid_k)
        mats.append(q_side | k_side)
    return np.tile(np.concatenate(mats, axis=0).astype(np.float32), (1, HG_SPLIT))


def _hgrn_kernel(zh_ref, s0_ref, lb_ref, ng_ref, dm_ref, ya_ref, sout_ref, st_ref,
                 *, chunk, n_inner, valid_len, n_seq):
    nh, dk, dv = HG_HEADS, HG_DK, HG_DV
    nf = nh * dk
    j = pl.program_id(1)

    @pl.when(j == 0)
    def _():
        for s in range(n_seq):
            for h in range(nh):
                st_ref[s, h] = s0_ref[s, h].T

    lb = lb_ref[...]
    ng = ng_ref[...]
    dmat = dm_ref[...]
    ti = lax.broadcasted_iota(jnp.int32, (chunk, chunk), 0)
    si = lax.broadcasted_iota(jnp.int32, (chunk, chunk), 1)
    xs = jnp.where(ti > si, jnp.bitwise_xor(ti, si), 0)
    levels = _hgrn_levels(chunk)

    def one_chunk(ci, carry):
        for s in range(n_seq):
            seq_chunk(s, pl.multiple_of(ci * chunk, chunk))
        return carry

    def seq_chunk(s, r0):
        zh = zh_ref[s, pl.ds(r0, chunk), :]
        hq, hf = zh[:, :nf], zh[:, nf:2 * nf]
        hv, hg = zh[:, 2 * nf:2 * nf + nh * dv], zh[:, 2 * nf + nh * dv:]
        logf = jnp.log(lb + (1.0 - lb) * jax.nn.sigmoid(hf))
        kk = (1.0 - lb) * jax.nn.sigmoid(-hf)
        if valid_len < chunk:
            live = lax.broadcasted_iota(jnp.int32, (chunk, 1), 0) < valid_len
            logf = jnp.where(live, logf, 0.0)
            kk = jnp.where(live, kk, 0.0)
        qf = hq * jax.nn.sigmoid(hq)
        pieces, rest = [], logf
        for _ in range(HG_SPLIT):
            piece = rest.astype(BF16)
            pieces.append(piece)
            rest = rest - piece.astype(F32)
        dec = _dot(dmat, jnp.concatenate(pieces, axis=0))
        outs = []
        for h in range(nh):
            ks = slice(h * dk, (h + 1) * dk)
            vs = slice(h * dv, (h + 1) * dv)
            q_h, k_h, v_h = qf[:, ks], kk[:, ks], hv[:, vs]
            b = dec[0:chunk, ks]
            after = b[chunk - 1:chunk, :] - b
            st = st_ref[s, h]
            o = _dot_nt((q_h * jnp.exp(b)).astype(BF16), st.astype(BF16))
            att = None
            for li, m in enumerate(levels):
                w = jnp.exp(dec[(1 + li) * chunk:(2 + li) * chunk, ks])
                a_l = _dot_nt((q_h * w).astype(BF16), (k_h * w).astype(BF16))
                att = a_l if att is None else jnp.where(xs >= m, a_l, att)
            diag = jnp.sum(q_h * k_h, axis=-1, keepdims=True)
            att = jnp.where(xs >= 1, att, 0.0)
            att = jnp.where(ti == si, diag, att)
            o = o + _dot(att.astype(BF16), v_h.astype(BF16))
            kd = (k_h * jnp.exp(after)).astype(BF16)
            st_ref[s, h] = jnp.exp(b[chunk - 1:chunk, :]) * st + _dot(v_h.T.astype(BF16), kd)
            o = o * lax.rsqrt(jnp.mean(o * o, axis=-1, keepdims=True) + EPS) * ng
            g_h = hg[:, vs]
            outs.append((o * (g_h * jax.nn.sigmoid(g_h))).astype(BF16))
        ya_ref[s, pl.ds(r0, chunk), :] = jnp.concatenate(outs, axis=-1)

    lax.fori_loop(0, n_inner, one_chunk, 0, unroll=min(HG_UNROLL, n_inner))

    @pl.when(j == pl.num_programs(1) - 1)
    def _():
        for s in range(n_seq):
            for h in range(nh):
                sout_ref[s, h] = st_ref[s, h].T


def _hgrn(zh, s0, lb, ng, valid_len):
    bsz, seq, width = zh.shape
    chunk = min(HG_CHUNK, seq)
    step = min(HG_STEP, seq)
    assert seq % step == 0 and step % chunk == 0 and chunk % 16 == 0
    dmat = jnp.asarray(_hgrn_decay_matrix(chunk), BF16)
    nv = HG_HEADS * HG_DV
    n_seq = math.gcd(bsz, HG_SHORT_SEQS) if seq == chunk else 1
    st_spec = pl.BlockSpec((n_seq, HG_HEADS, HG_DK, HG_DV), lambda b, j: (b, 0, 0, 0))
    return pl.pallas_call(
        functools.partial(_hgrn_kernel, chunk=chunk, n_inner=step // chunk, valid_len=min(valid_len, chunk),
                          n_seq=n_seq),
        grid=(bsz // n_seq, seq // step),
        in_specs=[pl.BlockSpec((n_seq, step, width), lambda b, j: (b, j, 0)), st_spec,
                  _const_spec(lb.shape), _const_spec(ng.shape), _const_spec(dmat.shape)],
        out_specs=[pl.BlockSpec((n_seq, step, nv), lambda b, j: (b, j, 0)), st_spec],
        out_shape=[jax.ShapeDtypeStruct((bsz, seq, nv), BF16),
                   jax.ShapeDtypeStruct((bsz, HG_HEADS, HG_DK, HG_DV), F32)],
        scratch_shapes=[pltpu.VMEM((n_seq, HG_HEADS, HG_DV, HG_DK), F32)],
        compiler_params=pltpu.CompilerParams(dimension_semantics=("parallel", "arbitrary"),
                                             vmem_limit_bytes=VMEM_SMALL),
        name="hgrn",
    )(zh, s0, lb, ng, dmat)


def _rel_bucket(dist):
    n = jnp.maximum(dist, 0)
    nf = jnp.maximum(n, 1).astype(F32)
    large = MAX_EXACT + (jnp.log(nf / MAX_EXACT) / math.log(MAX_DISTANCE / MAX_EXACT)
                         * (N_BUCKETS - MAX_EXACT)).astype(jnp.int32)
    large = jnp.minimum(large, N_BUCKETS - 1)
    return jnp.where(n < MAX_EXACT, n, large)


def _top_mask(gs, lane, n_sel):
    mask = jnp.zeros(gs.shape, F32)
    lane_f = lane.astype(F32)
    cur = gs
    for _ in range(n_sel):
        best = jnp.max(cur, axis=-1, keepdims=True)
        idx = jnp.min(jnp.where(cur == best, lane_f, float(gs.shape[-1])), axis=-1, keepdims=True)
        hit = lane_f == idx
        mask = jnp.where(hit, 1.0, mask)
        cur = jnp.where(hit, -jnp.inf, cur)
    return mask


def _moba_prompt_kernel(th_ref, rb_ref, q_ref, kt_ref, vt_ref, o_ref, bias_sc, *, n_blocks):
    blk, hd = MB_BLOCK, MB_HD
    heads = q_ref.shape[-1] // hd
    seq = q_ref.shape[1]
    log2e = 1.0 / math.log(2.0)
    scale = hd ** -0.5 * log2e
    grp = pl.program_id(0)
    ti = lax.broadcasted_iota(jnp.int32, (blk, blk), 0)
    si = lax.broadcasted_iota(jnp.int32, (blk, blk), 1)
    causal = ti >= si

    @pl.when(pl.program_id(1) == 0)
    def _():
        for delta in range(2):
            d = jnp.maximum(delta * blk + ti - si, 0)
            for hl in range(heads):
                base = (grp * heads + hl) * N_BUCKETS
                far = rb_ref[base + N_BUCKETS - 1]
                val = jnp.full((blk, blk), far, F32)
                for bk in range(N_BUCKETS - 2, -1, -1):
                    val = jnp.where(d < th_ref[bk + 1], rb_ref[base + bk], val)
                bias_sc[delta, hl] = (val - far) * log2e

    row = lax.broadcasted_iota(jnp.int32, (hd, seq), 0)
    col_blk = lax.broadcasted_iota(jnp.int32, (hd, seq), 1) // blk
    indicator = jnp.where(row == col_blk, 1.0, 0.0).astype(BF16)
    ones_row = jnp.where(row == 0, 1.0, 0.0).astype(BF16)

    nb8 = -(-n_blocks // 8) * 8
    blk_idx = lax.broadcasted_iota(jnp.int32, (nb8, seq), 0)
    q_blk = lax.broadcasted_iota(jnp.int32, (nb8, seq), 1) // blk
    fully_past = blk_idx < q_blk

    q_all = q_ref[0]
    qs, ktx, vtx = [], [], []
    for hl in range(heads):
        q = q_all[:, hl * hd:(hl + 1) * hd]
        kt = kt_ref[0, hl * hd:(hl + 1) * hd, :]
        ktx.append(jnp.concatenate([kt.astype(BF16), indicator], axis=0))
        vtx.append(jnp.concatenate([vt_ref[0, hl * hd:(hl + 1) * hd, :].astype(BF16), ones_row], axis=0))
        if n_blocks > 1:
            lane_k = lax.broadcasted_iota(jnp.int32, (hd, 128), 1)
            kmean = jnp.zeros((hd, 128), F32)
            for n in range(n_blocks):
                ksum = jnp.sum(kt[:, n * blk:(n + 1) * blk], axis=-1, keepdims=True)
                kmean = jnp.where(lane_k == n, ksum * (1.0 / blk), kmean)
            gate = _dot_precise(q, kmean).T[:nb8]
            gate = jnp.where(fully_past, gate, -jnp.inf)
            rank = jnp.zeros((nb8, seq), F32)
            for r in range(1, nb8):
                other, other_idx = pltpu.roll(gate, r, 0), pltpu.roll(blk_idx, r, 0)
                beats = (other > gate) | ((other == gate) & (other_idx < blk_idx))
                rank = rank + jnp.where(beats, 1.0, 0.0)
            keep = (fully_past & (rank < MB_TOPK)) | (blk_idx >= q_blk)
            pen_t = jnp.concatenate([jnp.where(keep, 0.0, NEG), jnp.zeros((128 - nb8, seq), F32)], axis=0)
            pen = pen_t.T[:, :hd].astype(BF16)
        else:
            pen = jnp.zeros((seq, hd), BF16)
        qs.append(jnp.concatenate([(q * scale).astype(BF16), pen], axis=-1))

    for i in range(n_blocks):
        rows = slice(i * blk, (i + 1) * blk)
        outs = []
        for hl in range(heads):
            s_all = _dot(qs[hl][rows], ktx[hl][:, :(i + 1) * blk])
            tiles = []
            for jb in range(i + 1):
                s = s_all[:, jb * blk:(jb + 1) * blk]
                if jb == i:
                    s = jnp.where(causal, s + bias_sc[0, hl], NEG)
                elif jb == i - 1:
                    s = s + bias_sc[1, hl]
                tiles.append(s)
            mx = tiles[0]
            for s in tiles[1:]:
                mx = jnp.maximum(mx, s)
            mx = jnp.max(mx, axis=-1, keepdims=True)
            p = jnp.concatenate([jnp.exp2(s - mx).astype(BF16) for s in tiles], axis=-1)
            acc = _dot_nt(p, vtx[hl][:, :(i + 1) * blk])
            outs.append((acc[:, :hd] / acc[:, hd:hd + 1]).astype(BF16))
        o_ref[0, rows, :] = jnp.concatenate(outs, axis=-1)


def _bucket_thresholds():
    buckets = _rel_bucket(jnp.arange(MAX_DISTANCE + 1))
    return jnp.sum(buckets[None, :] < jnp.arange(N_BUCKETS + 1)[:, None], axis=1).astype(jnp.int32)


def _moba_prompt(mq, kt, vt, rel_bias):
    bsz, seq, width = mq.shape
    assert seq % MB_BLOCK == 0 and MB_BLOCK >= MAX_DISTANCE and seq // MB_BLOCK <= MB_HD
    hp = 128 // MB_HD
    n_groups = width // 128
    smem = pl.BlockSpec(memory_space=pltpu.SMEM)
    return pl.pallas_call(
        functools.partial(_moba_prompt_kernel, n_blocks=seq // MB_BLOCK),
        grid=(n_groups, bsz),
        in_specs=[smem, smem,
                  pl.BlockSpec((1, seq, 128), lambda g, b: (b, 0, g)),
                  pl.BlockSpec((1, 128, seq), lambda g, b: (b, g, 0)),
                  pl.BlockSpec((1, 128, seq), lambda g, b: (b, g, 0))],
        out_specs=pl.BlockSpec((1, seq, 128), lambda g, b: (b, 0, g)),
        out_shape=jax.ShapeDtypeStruct((bsz, seq, width), BF16),
        scratch_shapes=[pltpu.VMEM((2, hp, MB_BLOCK, MB_BLOCK), F32)],
        compiler_params=pltpu.CompilerParams(dimension_semantics=("parallel", "arbitrary"),
                                             vmem_limit_bytes=VMEM_SMALL),
        name="moba_prompt",
    )(_bucket_thresholds(), rel_bias.T.reshape(-1).astype(F32), mq, kt, vt)


def _moba_sample_kernel(pt_ref, q_ref, kown_ref, vown_ref, bfar_ref, blast_ref, bown_ref, ck_ref, cv_ref,
                        o_ref, buf, sem, s_all, *, n_pages, pps, layer):
    nh, hd = MB_HEADS, MB_HD
    n_tok = q_ref.shape[1]
    rows = nh * n_tok
    width = nh * hd
    ppb = MB_BLOCK // PAGE_SIZE
    n_full = n_pages // ppb
    gk = n_pages // pps
    n_groups = 2 * gk
    n_slots = buf.shape[0]
    ahead = n_slots - 1
    seq_id = pl.program_id(0)
    n_seq = pl.num_programs(0)

    def group_copies(slot, cache_ref, page_of):
        return [pltpu.make_async_copy(cache_ref.at[layer, page_of(jj)], buf.at[slot, jj], sem.at[slot])
                for jj in range(pps)]

    def start_group(seq, loc):
        slot = (seq * n_groups + loc) % n_slots

        @pl.when(loc < gk)
        def _():
            for cp in group_copies(slot, ck_ref, lambda jj: pt_ref[seq * n_pages + loc * pps + jj]):
                cp.start()

        @pl.when(loc >= gk)
        def _():
            for cp in group_copies(slot, cv_ref, lambda jj: pt_ref[seq * n_pages + (loc - gk) * pps + jj]):
                cp.start()

    def prefetch(loc):
        nxt = loc + ahead

        @pl.when(nxt < n_groups)
        def _():
            start_group(seq_id, nxt)

        @pl.when((nxt >= n_groups) & (seq_id + 1 < n_seq))
        def _():
            start_group(seq_id + 1, nxt - n_groups)

    def wait_group(loc):
        slot = (seq_id * n_groups + loc) % n_slots
        for cp in group_copies(slot, ck_ref, lambda jj: 0):
            cp.wait()
        return slot

    @pl.when(seq_id == 0)
    def _():
        for loc in range(ahead):
            start_group(seq_id, jnp.int32(loc))

    q = q_ref[0]
    rh = lax.broadcasted_iota(jnp.int32, (rows, width), 0) // n_tok
    ch = lax.broadcasted_iota(jnp.int32, (rows, width), 1) // hd
    own_head = rh == ch
    q_rep = jnp.concatenate([q] * nh, axis=0)
    qbd = jnp.where(own_head, q_rep * (hd ** -0.5), 0.0).astype(BF16)
    lane = lax.broadcasted_iota(jnp.int32, (rows, PAGE_SIZE), 1)

    lane_w = lax.broadcasted_iota(jnp.int32, (width, PAGE_SIZE), 1)

    def key_group(g, carry):
        pm, ms = carry
        slot = wait_group(g)
        prefetch(g)
        for jb in range(pps // ppb):
            ksum = None
            for pj in range(ppb):
                jj = jb * ppb + pj
                kp = buf[slot, jj].reshape(width, PAGE_SIZE)
                page = g * pps + jj
                s = _dot(qbd, kp.astype(BF16))
                s_all[page] = s
                pm = jnp.where(lane == page, jnp.max(s, axis=-1, keepdims=True), pm)
                ksum = kp if ksum is None else ksum + kp
            ms = jnp.where(lane_w == g * (pps // ppb) + jb, jnp.sum(ksum, axis=-1, keepdims=True), ms)
        return pm, ms

    pm, ms = lax.fori_loop(0, gk, key_group, (jnp.full((rows, PAGE_SIZE), -jnp.inf, F32),
                                              jnp.zeros((width, PAGE_SIZE), F32)))

    q_bd32 = jnp.where(own_head, q_rep, 0.0)
    gs = _dot_precise(q_bd32, ms * (1.0 / MB_BLOCK))
    gs = jnp.where(lane < n_full, gs, -jnp.inf)
    sel = _top_mask(gs, lane, min(MB_TOPK, n_full))
    blk_of_page = lax.broadcasted_iota(jnp.int32, (PAGE_SIZE, PAGE_SIZE), 1) // ppb
    blk_row = lax.broadcasted_iota(jnp.int32, (PAGE_SIZE, PAGE_SIZE), 0)
    sel_page = _dot(sel.astype(BF16), jnp.where(blk_row == blk_of_page, 1.0, 0.0).astype(BF16)) > 0.5
    bfar = bfar_ref[...]
    for pj in range(ppb):
        page = n_pages - ppb + pj
        s = s_all[page] + (blast_ref[:, pj * PAGE_SIZE:(pj + 1) * PAGE_SIZE] - bfar)
        s_all[page] = s
        pm = jnp.where(lane == page, jnp.max(s, axis=-1, keepdims=True), pm)

    pad = jnp.zeros((PAGE_SIZE - n_tok, width), F32)
    kown = jnp.concatenate([kown_ref[0], pad], axis=0).astype(BF16)
    vown = jnp.concatenate([vown_ref[0], pad], axis=0).astype(BF16)
    tq = lax.broadcasted_iota(jnp.int32, (rows, PAGE_SIZE), 0) % n_tok
    s_own = jnp.where(lane <= tq, _dot_nt(qbd, kown) + (bown_ref[...] - bfar), -jnp.inf)
    mx = jnp.max(jnp.maximum(jnp.where(sel_page, pm, -jnp.inf), s_own), axis=-1, keepdims=True)
    shift = jnp.where(sel_page, mx, jnp.inf)
    p_own = jnp.exp(s_own - mx)

    def value_group(g, carry):
        den, acc = carry
        slot = wait_group(gk + g)
        prefetch(gk + g)
        for jj in range(pps):
            page = g * pps + jj
            sh = jnp.min(jnp.where(lane == page, shift, jnp.inf), axis=-1, keepdims=True)
            p = jnp.exp(s_all[page] - sh)
            den = den + p
            acc = acc + _dot_nt(p.astype(BF16), buf[slot, jj].reshape(width, PAGE_SIZE).astype(BF16))
        return den, acc

    den, acc = lax.fori_loop(0, gk, value_group, (p_own, _dot(p_own.astype(BF16), vown)))
    o = jnp.where(own_head, acc / jnp.sum(den, axis=-1, keepdims=True), 0.0)
    out = o[0:n_tok]
    for h in range(1, nh):
        out = out + o[h * n_tok:(h + 1) * n_tok]
    o_ref[0] = out


def _moba_sample(mq, kown, vown, ck_t, cv_t, page_table, rel_bias, layer):
    dbs, n_tok, width = mq.shape
    n_pages = page_table.shape[1]
    past = n_pages * PAGE_SIZE
    ppb = MB_BLOCK // PAGE_SIZE
    pps = PAGES_PER_STEP
    assert past % MB_BLOCK == 0 and n_pages % pps == 0 and pps % ppb == 0 and n_pages // ppb >= MB_TOPK
    assert n_tok <= MAX_EXACT and MB_BLOCK >= MAX_DISTANCE and n_tok % 8 == 0 and n_pages <= PAGE_SIZE
    rows = MB_HEADS * n_tok
    gk = n_pages // pps
    t = jnp.arange(n_tok)
    d_last = (t[:, None] + 1) + jnp.arange(MB_BLOCK)[::-1][None, :]
    blast = rel_bias[_rel_bucket(d_last)].transpose(2, 0, 1).reshape(rows, MB_BLOCK)
    bfar = jnp.broadcast_to(rel_bias[_rel_bucket(jnp.asarray(MB_BLOCK + 1))][:, None, None],
                            (MB_HEADS, n_tok, PAGE_SIZE)).reshape(rows, PAGE_SIZE)
    d_own = t[:, None] - jnp.arange(PAGE_SIZE)[None, :]
    bown = rel_bias[_rel_bucket(d_own)].transpose(2, 0, 1).reshape(rows, PAGE_SIZE)

    tok_spec = pl.BlockSpec((1, n_tok, width), lambda b, pt: (b, 0, 0))
    full = lambda a: pl.BlockSpec(a.shape, lambda b, pt: (0,) * a.ndim)
    hbm = pl.BlockSpec(memory_space=pl.ANY)
    grid_spec = pltpu.PrefetchScalarGridSpec(
        num_scalar_prefetch=1,
        grid=(dbs,),
        in_specs=[tok_spec, tok_spec, tok_spec, full(bfar), full(blast), full(bown), hbm, hbm],
        out_specs=tok_spec,
        scratch_shapes=[pltpu.VMEM((RING_SLOTS, pps, MB_HEADS, MB_HD, PAGE_SIZE), F32),
                        pltpu.SemaphoreType.DMA((RING_SLOTS,)),
                        pltpu.VMEM((n_pages, rows, PAGE_SIZE), F32)],
    )
    return pl.pallas_call(
        functools.partial(_moba_sample_kernel, n_pages=n_pages, pps=pps, layer=layer),
        grid_spec=grid_spec,
        out_shape=jax.ShapeDtypeStruct((dbs, n_tok, width), F32),
        compiler_params=pltpu.CompilerParams(dimension_semantics=("arbitrary",),
                                             vmem_limit_bytes=VMEM_SMALL),
        name="moba_sample",
    )(page_table.reshape(-1), mq, kown, vown, bfar, blast, bown, ck_t, cv_t)


def kernel(x_prompt, x_sample, state_hgrn, cache_k, cache_v, page_table, ffn1_norm, ffn1_w_gate, ffn1_w_up,
           ffn1_w_down, mix_norm, w_in, hg_lb, hg_norm, rel_bias, w_branch_a, w_branch_b, w_out, ffn2_norm,
           ffn2_w_gate, ffn2_w_up, ffn2_w_down, final_norm):
    depth = w_in.shape[0]
    assert depth == 1, "the final norm is fused into the (single) layer's last kernel"
    bsz, seq, d = x_prompt.shape
    dbs, n_tok, _ = x_sample.shape
    nf, nv, mw = HG_HEADS * HG_DK, HG_HEADS * HG_DV, MB_HEADS * MB_HD
    n_hg = 2 * nf + 2 * nv
    assert w_in.shape[2] == n_hg + 3 * mw + 2 * d

    lbs = jnp.cumsum(jax.nn.softmax(hg_lb.astype(F32), axis=0), axis=0)
    ck_t = jnp.swapaxes(cache_k, 3, 4)
    cv_t = jnp.swapaxes(cache_v, 3, 4)
    final_g = final_norm.reshape(1, d)

    xp = x_prompt
    xs = x_sample.reshape(1, dbs * n_tok, d)
    sp_l, ss_l, kp_l, vp_l, ks_l, vs_l = [], [], [], [], [], []
    yp = ys = None
    for l in range(depth):
        w = w_in[l]
        k0 = n_hg + mw
        ffn_in_w = (ffn1_norm[l].reshape(1, d), ffn1_w_gate[l].astype(BF16), ffn1_w_up[l].astype(BF16),
                    ffn1_w_down[l].astype(BF16), mix_norm[l].reshape(1, d), w[:, :k0].astype(BF16),
                    w[:, k0:k0 + mw].T.astype(BF16), w[:, k0 + mw:k0 + 2 * mw].T.astype(BF16),
                    w[:, k0 + 2 * mw:].astype(BF16))
        merge_w = (w_branch_a[l].astype(BF16), w_branch_b[l].astype(BF16), w_out[l].astype(BF16),
                   ffn2_norm[l].reshape(1, d), ffn2_w_gate[l].astype(BF16), ffn2_w_up[l].astype(BF16),
                   ffn2_w_down[l].astype(BF16), final_g)
        lb = lbs[l].reshape(1, nf)
        ng = hg_norm[l].reshape(1, HG_DV)

        x1p, zhp, mqp, ktp, vtp, gabp = _ffn_in(xp, *ffn_in_w)
        ya_p, st_p = _hgrn(zhp, jnp.zeros((bsz, HG_HEADS, HG_DK, HG_DV), F32), lb, ng, seq)
        yb_p = _moba_prompt(mqp, ktp, vtp, rel_bias)
        yp = _merge_ffn(x1p, ya_p, yb_p, gabp, *merge_w)

        x1s, zhs, mqs, kts, vts, gabs = _ffn_in(xs, *ffn_in_w)
        pad_tok = max(16, n_tok)
        zhs_b = jnp.pad(zhs.reshape(dbs, n_tok, n_hg), ((0, 0), (0, pad_tok - n_tok), (0, 0)))
        ya_s, st_s = _hgrn(zhs_b, state_hgrn[l], lb, ng, n_tok)
        ya_s = ya_s[:, :n_tok].reshape(1, dbs * n_tok, nv)
        kown = kts[0].T.reshape(dbs, n_tok, mw)
        vown = vts[0].T.reshape(dbs, n_tok, mw)
        yb_s = _moba_sample(mqs.reshape(dbs, n_tok, mw), kown, vown, ck_t, cv_t, page_table, rel_bias, l)
        ys = _merge_ffn(x1s, ya_s, yb_s.reshape(1, dbs * n_tok, mw), gabs, *merge_w)

        sp_l.append(st_p)
        ss_l.append(st_s)
        kp_l.append(jnp.swapaxes(ktp.reshape(bsz, MB_HEADS, MB_HD, seq), 2, 3))
        vp_l.append(jnp.swapaxes(vtp.reshape(bsz, MB_HEADS, MB_HD, seq), 2, 3))
        ks_l.append(kown.reshape(dbs, n_tok, MB_HEADS, MB_HD).transpose(0, 2, 1, 3))
        vs_l.append(vown.reshape(dbs, n_tok, MB_HEADS, MB_HD).transpose(0, 2, 1, 3))
    return (yp, ys.reshape(dbs, n_tok, d), jnp.stack(sp_l), jnp.stack(ss_l), jnp.stack(kp_l), jnp.stack(vp_l),
            jnp.stack(ks_l), jnp.stack(vs_l))
```

```python
import functools
import math

import numpy as np
import jax
import jax.numpy as jnp
from jax import lax
from jax.experimental import pallas as pl
from jax.experimental.pallas import tpu as pltpu

F32 = jnp.float32
BF16 = jnp.bfloat16

EPS = 1e-6
NEG = -1e30
HG_HEADS = 4
HG_DK = 128
HG_DV = 128
MB_HEADS = 8
MB_HD = 64
MB_BLOCK = 256
MB_TOPK = 3
PAGE_SIZE = 128
N_BUCKETS = 32
MAX_EXACT = N_BUCKETS // 2
MAX_DISTANCE = 128

V7X_VMEM_BYTES = 64 * 2**20
MXU_DEPTH = 256
VMEM_BIG = V7X_VMEM_BYTES - 8 * 2**20
VMEM_SMALL = 40 * 2**20

ROW_TILE = 256
ROW_TILE_MERGE = 512
HG_CHUNK = 128
HG_STEP = 512
HG_UNROLL = 2
HG_SHORT_SEQS = 8
HG_LONG_SEQS = 2
MB_QROWS = 256
PAGES_PER_STEP = 8
RING_SLOTS = 4

_NT = (((1,), (1,)), ((), ()))


def _dot(a, b):
    return jnp.dot(a, b, preferred_element_type=F32)


def _dot_nt(a, b):
    return lax.dot_general(a, b, _NT, preferred_element_type=F32)


def _split2(a):
    hi = a.astype(BF16)
    lo = (a - hi.astype(F32)).astype(BF16)
    return hi, lo


def _dot_precise(a, b):
    a1, a2 = _split2(a)
    b1, b2 = _split2(b)
    if 3 * a.shape[-1] <= MXU_DEPTH:
        return _dot(jnp.concatenate([a1, a1, a2], axis=-1), jnp.concatenate([b1, b2, b1], axis=0))
    return _dot(a1, b1) + _dot(a1, b2) + _dot(a2, b1)


def _rms(x, g):
    return x * lax.rsqrt(jnp.mean(x * x, axis=-1, keepdims=True) + EPS) * g


def _swiglu(h, wg_ref, wu_ref, wd_ref, ff_chunk):
    d_ff = wg_ref.shape[1]
    acc = None
    for c0 in range(0, d_ff, ff_chunk):
        c1 = min(c0 + ff_chunk, d_ff)
        a = _dot(h, wg_ref[:, c0:c1])
        u = _dot(h, wu_ref[:, c0:c1])
        act = (a * jax.nn.sigmoid(a) * u).astype(BF16)
        part = _dot(act, wd_ref[c0:c1, :])
        acc = part if acc is None else acc + part
    return acc


def _ff_chunk(d_ff):
    return -(-d_ff // (2 * MXU_DEPTH)) * MXU_DEPTH if d_ff % MXU_DEPTH == 0 else d_ff


def _const_spec(shape):
    nd = len(shape)
    return pl.BlockSpec(shape, lambda *_: (0,) * nd, pipeline_mode=pl.Buffered(1))


def _ffn_in_kernel(x_ref, n1_ref, wg_ref, wu_ref, wd_ref, nm_ref, wz_ref, wkt_ref, wvt_ref, wgab_ref,
                   x1_ref, zh_ref, mq_ref, kt_ref, vt_ref, gab_ref, *, ff_chunk, n_hg):
    x = x_ref[0]
    h = _rms(x, n1_ref[...]).astype(BF16)
    x1 = x + 0.5 * _swiglu(h, wg_ref, wu_ref, wd_ref, ff_chunk)
    x1_ref[0] = x1
    h2 = _rms(x1, nm_ref[...]).astype(BF16)
    z = _dot(h2, wz_ref[...])
    zh_ref[0] = z[:, :n_hg]
    mq_ref[0] = z[:, n_hg:]
    kt_ref[0] = _dot_nt(wkt_ref[...], h2)
    vt_ref[0] = _dot_nt(wvt_ref[...], h2)
    gab_ref[0] = _dot(h2, wgab_ref[...]).astype(BF16)


def _ffn_in(x, n1, wg, wu, wd, nm, wz, wkt, wvt, wgab):
    bsz, seq, d = x.shape
    tm = min(ROW_TILE, seq)
    assert seq % tm == 0
    n_z = wz.shape[1]
    n_hg = 2 * HG_HEADS * HG_DK + 2 * HG_HEADS * HG_DV
    n_kv = wkt.shape[0]
    n_gab = wgab.shape[1]
    row = lambda w: pl.BlockSpec((1, tm, w), lambda b, i: (b, i, 0))
    col = pl.BlockSpec((1, n_kv, tm), lambda b, i: (b, 0, i))
    outs = pl.pallas_call(
        functools.partial(_ffn_in_kernel, ff_chunk=_ff_chunk(wg.shape[1]), n_hg=n_hg),
        grid=(bsz, seq // tm),
        in_specs=[row(d), _const_spec(n1.shape), _const_spec(wg.shape), _const_spec(wu.shape),
                  _const_spec(wd.shape), _const_spec(nm.shape), _const_spec(wz.shape),
                  _const_spec(wkt.shape), _const_spec(wvt.shape), _const_spec(wgab.shape)],
        out_specs=[row(d), row(n_hg), row(n_z - n_hg), col, col, row(n_gab)],
        out_shape=[jax.ShapeDtypeStruct((bsz, seq, d), F32),
                   jax.ShapeDtypeStruct((bsz, seq, n_hg), F32),
                   jax.ShapeDtypeStruct((bsz, seq, n_z - n_hg), F32),
                   jax.ShapeDtypeStruct((bsz, n_kv, seq), F32),
                   jax.ShapeDtypeStruct((bsz, n_kv, seq), F32),
                   jax.ShapeDtypeStruct((bsz, seq, n_gab), BF16)],
        compiler_params=pltpu.CompilerParams(dimension_semantics=("parallel", "parallel"),
                                             vmem_limit_bytes=VMEM_BIG),
        name="ffn_in",
    )(x, n1, wg, wu, wd, nm, wz, wkt, wvt, wgab)
    return outs


def _merge_ffn_kernel(x1_ref, ya_ref, yb_ref, gab_ref, wa_ref, wb_ref, wo_ref, n2_ref,
                      wg_ref, wu_ref, wd_ref, nf_ref, y_ref, *, ff_chunk):
    d = x1_ref.shape[-1]
    gab = gab_ref[0].astype(F32)
    ma = _dot(ya_ref[0].astype(BF16), wa_ref[...])
    mb = _dot(yb_ref[0].astype(BF16), wb_ref[...])
    m = jax.nn.sigmoid(gab[:, :d]) * ma + jax.nn.sigmoid(gab[:, d:]) * mb
    x2 = x1_ref[0] + _dot(m.astype(BF16), wo_ref[...])
    h = _rms(x2, n2_ref[...]).astype(BF16)
    x3 = x2 + 0.5 * _swiglu(h, wg_ref, wu_ref, wd_ref, ff_chunk)
    y_ref[0] = _rms(x3, nf_ref[...])


def _merge_ffn(x1, ya, yb, gab, wa, wb, wo, n2, wg, wu, wd, nf):
    bsz, seq, d = x1.shape
    tm = min(ROW_TILE_MERGE, seq)
    assert seq % tm == 0
    row = lambda w: pl.BlockSpec((1, tm, w), lambda b, i: (b, i, 0))
    return pl.pallas_call(
        functools.partial(_merge_ffn_kernel, ff_chunk=_ff_chunk(wg.shape[1])),
        grid=(bsz, seq // tm),
        in_specs=[row(d), row(ya.shape[-1]), row(yb.shape[-1]), row(gab.shape[-1]),
                  _const_spec(wa.shape), _const_spec(wb.shape), _const_spec(wo.shape), _const_spec(n2.shape),
                  _const_spec(wg.shape), _const_spec(wu.shape), _const_spec(wd.shape), _const_spec(nf.shape)],
        out_specs=row(d),
        out_shape=jax.ShapeDtypeStruct((bsz, seq, d), F32),
        compiler_params=pltpu.CompilerParams(dimension_semantics=("parallel", "parallel"),
                                             vmem_limit_bytes=VMEM_BIG),
        name="merge_ffn",
    )(x1, ya, yb, gab, wa, wb, wo, n2, wg, wu, wd, nf)


def _hgrn_levels(chunk):
    levels, m = [], 1
    while m < chunk:
        levels.append(m)
        m *= 2
    return levels


HG_SPLIT = 2


def _hgrn_decay_matrix(chunk):
    t = np.arange(chunk)[:, None]
    u = np.arange(chunk)[None, :]
    mats = [u <= t]
    for m in _hgrn_levels(chunk):
        mid_q = t & ~(m - 1)
        q_side = ((t & m) != 0) & (u >= mid_q) & (u <= t)
        mid_k = (t & ~(2 * m - 1)) + m
        k_side = ((t & m) == 0) & (u > t) & (u < mid_k)
        mats.append(q_side | k_side)
    return np.tile(np.concatenate(mats, axis=0).astype(np.float32), (1, HG_SPLIT))


def _hgrn_kernel(zh_ref, s0_ref, lb_ref, ng_ref, dm_ref, ya_ref, sout_ref, st_ref,
                 *, chunk, n_inner, valid_len, n_seq):
    nh, dk, dv = HG_HEADS, HG_DK, HG_DV
    nf = nh * dk
    j = pl.program_id(1)

    @pl.when(j == 0)
    def _():
        for s in range(n_seq):
            for h in range(nh):
                st_ref[s, h] = s0_ref[s, h].T

    lb = lb_ref[...]
    ng = ng_ref[...]
    dmat = dm_ref[...]
    ti = lax.broadcasted_iota(jnp.int32, (chunk, chunk), 0)
    si = lax.broadcasted_iota(jnp.int32, (chunk, chunk), 1)
    xs = jnp.where(ti > si, jnp.bitwise_xor(ti, si), 0)
    levels = _hgrn_levels(chunk)

    def one_chunk(ci, carry):
        for s in range(n_seq):
            seq_chunk(s, pl.multiple_of(ci * chunk, chunk))
        return carry

    def seq_chunk(s, r0):
        zh = zh_ref[s, pl.ds(r0, chunk), :]
        hq, hf = zh[:, :nf], zh[:, nf:2 * nf]
        hv, hg = zh[:, 2 * nf:2 * nf + nh * dv], zh[:, 2 * nf + nh * dv:]
        f = lb + (1.0 - lb) * jax.nn.sigmoid(hf)
        logf = jnp.log2(f)
        kk = 1.0 - f
        if valid_len < chunk:
            live = lax.broadcasted_iota(jnp.int32, (chunk, 1), 0) < valid_len
            logf = jnp.where(live, logf, 0.0)
            kk = jnp.where(live, kk, 0.0)
        qf = hq * jax.nn.sigmoid(hq)
        pieces, rest = [], logf
        for _ in range(HG_SPLIT):
            piece = rest.astype(BF16)
            pieces.append(piece)
            rest = rest - piece.astype(F32)
        dec = _dot(dmat, jnp.concatenate(pieces, axis=0))
        outs = []
        for h in range(nh):
            ks = slice(h * dk, (h + 1) * dk)
            vs = slice(h * dv, (h + 1) * dv)
            q_h, k_h, v_h = qf[:, ks], kk[:, ks], hv[:, vs]
            b = dec[0:chunk, ks]
            after = b[chunk - 1:chunk, :] - b
            st = st_ref[s, h]
            o = _dot_nt((q_h * jnp.exp2(b)).astype(BF16), st.astype(BF16))
            att = None
            for li, m in enumerate(levels):
                w = jnp.exp2(dec[(1 + li) * chunk:(2 + li) * chunk, ks])
                a_l = _dot_nt((q_h * w).astype(BF16), (k_h * w).astype(BF16))
                att = a_l if att is None else jnp.where(xs >= m, a_l, att)
            diag = jnp.sum(q_h * k_h, axis=-1, keepdims=True)
            att = jnp.where(xs >= 1, att, 0.0)
            att = jnp.where(ti == si, diag, att)
            o = o + _dot(att.astype(BF16), v_h.astype(BF16))
            kd = (k_h * jnp.exp2(after)).astype(BF16)
            st_ref[s, h] = jnp.exp2(b[chunk - 1:chunk, :]) * st + _dot(v_h.T.astype(BF16), kd)
            o = o * lax.rsqrt(jnp.mean(o * o, axis=-1, keepdims=True) + EPS) * ng
            g_h = hg[:, vs]
            outs.append((o * (g_h * jax.nn.sigmoid(g_h))).astype(BF16))
        ya_ref[s, pl.ds(r0, chunk), :] = jnp.concatenate(outs, axis=-1)

    lax.fori_loop(0, n_inner, one_chunk, 0, unroll=min(HG_UNROLL, n_inner))

    @pl.when(j == pl.num_programs(1) - 1)
    def _():
        for s in range(n_seq):
            for h in range(nh):
                sout_ref[s, h] = st_ref[s, h].T


def _hgrn(zh, s0, lb, ng, valid_len):
    bsz, seq, width = zh.shape
    chunk = min(HG_CHUNK, seq)
    step = min(HG_STEP, seq)
    assert seq % step == 0 and step % chunk == 0 and chunk % 16 == 0
    dmat = jnp.asarray(_hgrn_decay_matrix(chunk), BF16)
    nv = HG_HEADS * HG_DV
    n_seq = math.gcd(bsz, HG_SHORT_SEQS if seq == chunk else HG_LONG_SEQS)
    st_spec = pl.BlockSpec((n_seq, HG_HEADS, HG_DK, HG_DV), lambda b, j: (b, 0, 0, 0))
    return pl.pallas_call(
        functools.partial(_hgrn_kernel, chunk=chunk, n_inner=step // chunk, valid_len=min(valid_len, chunk),
                          n_seq=n_seq),
        grid=(bsz // n_seq, seq // step),
        in_specs=[pl.BlockSpec((n_seq, step, width), lambda b, j: (b, j, 0)), st_spec,
                  _const_spec(lb.shape), _const_spec(ng.shape), _const_spec(dmat.shape)],
        out_specs=[pl.BlockSpec((n_seq, step, nv), lambda b, j: (b, j, 0)), st_spec],
        out_shape=[jax.ShapeDtypeStruct((bsz, seq, nv), BF16),
                   jax.ShapeDtypeStruct((bsz, HG_HEADS, HG_DK, HG_DV), F32)],
        scratch_shapes=[pltpu.VMEM((n_seq, HG_HEADS, HG_DV, HG_DK), F32)],
        compiler_params=pltpu.CompilerParams(dimension_semantics=("parallel", "arbitrary"),
                                             vmem_limit_bytes=VMEM_SMALL),
        name="hgrn",
    )(zh, s0, lb, ng, dmat)


def _rel_bucket(dist):
    n = jnp.maximum(dist, 0)
    nf = jnp.maximum(n, 1).astype(F32)
    large = MAX_EXACT + (jnp.log(nf / MAX_EXACT) / math.log(MAX_DISTANCE / MAX_EXACT)
                         * (N_BUCKETS - MAX_EXACT)).astype(jnp.int32)
    large = jnp.minimum(large, N_BUCKETS - 1)
    return jnp.where(n < MAX_EXACT, n, large)


def _top_mask(gs, lane, n_sel):
    mask = jnp.zeros(gs.shape, F32)
    lane_f = lane.astype(F32)
    cur = gs
    for _ in range(n_sel):
        best = jnp.max(cur, axis=-1, keepdims=True)
        idx = jnp.min(jnp.where(cur == best, lane_f, float(gs.shape[-1])), axis=-1, keepdims=True)
        hit = lane_f == idx
        mask = jnp.where(hit, 1.0, mask)
        cur = jnp.where(hit, -jnp.inf, cur)
    return mask


def _moba_prompt_kernel(th_ref, rb_ref, q_ref, kt_ref, vt_ref, o_ref, bias_sc, *, n_blocks):
    blk, hd = MB_BLOCK, MB_HD
    heads = q_ref.shape[-1] // hd
    seq = q_ref.shape[1]
    log2e = 1.0 / math.log(2.0)
    scale = hd ** -0.5 * log2e
    grp = pl.program_id(0)
    ti = lax.broadcasted_iota(jnp.int32, (blk, blk), 0)
    si = lax.broadcasted_iota(jnp.int32, (blk, blk), 1)
    causal = ti >= si

    @pl.when(pl.program_id(1) == 0)
    def _():
        for delta in range(2):
            d = jnp.maximum(delta * blk + ti - si, 0)
            for hl in range(heads):
                base = (grp * heads + hl) * N_BUCKETS
                far = rb_ref[base + N_BUCKETS - 1]
                val = jnp.full((blk, blk), far, F32)
                for bk in range(N_BUCKETS - 2, -1, -1):
                    val = jnp.where(d < th_ref[bk + 1], rb_ref[base + bk], val)
                bias_sc[delta, hl] = (val - far) * log2e

    row = lax.broadcasted_iota(jnp.int32, (hd, seq), 0)
    col_blk = lax.broadcasted_iota(jnp.int32, (hd, seq), 1) // blk
    indicator = jnp.where(row == col_blk, 1.0, 0.0).astype(BF16)
    ones_row = jnp.where(row == 0, 1.0, 0.0).astype(BF16)

    nb8 = -(-n_blocks // 8) * 8
    blk_idx = lax.broadcasted_iota(jnp.int32, (nb8, seq), 0)
    q_blk = lax.broadcasted_iota(jnp.int32, (nb8, seq), 1) // blk
    fully_past = blk_idx < q_blk

    q_all = q_ref[0]
    qs, ktx, vtx = [], [], []
    for hl in range(heads):
        q = q_all[:, hl * hd:(hl + 1) * hd]
        kt = kt_ref[0, hl * hd:(hl + 1) * hd, :]
        ktx.append(jnp.concatenate([kt.astype(BF16), indicator], axis=0))
        vtx.append(jnp.concatenate([vt_ref[0, hl * hd:(hl + 1) * hd, :].astype(BF16), ones_row], axis=0))
        if n_blocks > 1:
            lane_k = lax.broadcasted_iota(jnp.int32, (hd, 128), 1)
            kmean = jnp.zeros((hd, 128), F32)
            for n in range(n_blocks):
                ksum = jnp.sum(kt[:, n * blk:(n + 1) * blk], axis=-1, keepdims=True)
                kmean = jnp.where(lane_k == n, ksum * (1.0 / blk), kmean)
            gate = _dot_precise(q, kmean).T[:nb8]
            gate = jnp.where(fully_past, gate, -jnp.inf)
            rank = jnp.zeros((nb8, seq), F32)
            for r in range(1, nb8):
                other, other_idx = pltpu.roll(gate, r, 0), pltpu.roll(blk_idx, r, 0)
                beats = (other > gate) | ((other == gate) & (other_idx < blk_idx))
                rank = rank + jnp.where(beats, 1.0, 0.0)
            keep = (fully_past & (rank < MB_TOPK)) | (blk_idx >= q_blk)
            pen_t = jnp.concatenate([jnp.where(keep, 0.0, NEG), jnp.zeros((128 - nb8, seq), F32)], axis=0)
            pen = pen_t.T[:, :hd].astype(BF16)
        else:
            pen = jnp.zeros((seq, hd), BF16)
        qs.append(jnp.concatenate([(q * scale).astype(BF16), pen], axis=-1))

    qr = MB_QROWS
    for i, sub in [(i, sub) for i in range(n_blocks) for sub in range(blk // qr)]:
        rows = slice(i * blk + sub * qr, i * blk + (sub + 1) * qr)
        in_blk = slice(sub * qr, (sub + 1) * qr)
        outs = []
        for hl in range(heads):
            s_all = _dot(qs[hl][rows], ktx[hl][:, :(i + 1) * blk])
            tiles = []
            for jb in range(i + 1):
                s = s_all[:, jb * blk:(jb + 1) * blk]
                if jb == i:
                    s = jnp.where(causal[in_blk], s + bias_sc[0, hl, in_blk, :], NEG)
                elif jb == i - 1:
                    s = s + bias_sc[1, hl, in_blk, :]
                tiles.append(s)
            mx = tiles[0]
            for s in tiles[1:]:
                mx = jnp.maximum(mx, s)
            mx = jnp.max(mx, axis=-1, keepdims=True)
            p = jnp.concatenate([jnp.exp2(s - mx).astype(BF16) for s in tiles], axis=-1)
            acc = _dot_nt(p, vtx[hl][:, :(i + 1) * blk])
            outs.append((acc[:, :hd] / acc[:, hd:hd + 1]).astype(BF16))
        o_ref[0, rows, :] = jnp.concatenate(outs, axis=-1)


def _bucket_thresholds():
    buckets = _rel_bucket(jnp.arange(MAX_DISTANCE + 1))
    return jnp.sum(buckets[None, :] < jnp.arange(N_BUCKETS + 1)[:, None], axis=1).astype(jnp.int32)


def _moba_prompt(mq, kt, vt, rel_bias):
    bsz, seq, width = mq.shape
    assert seq % MB_BLOCK == 0 and MB_BLOCK >= MAX_DISTANCE and seq // MB_BLOCK <= MB_HD
    hp = 128 // MB_HD
    n_groups = width // 128
    smem = pl.BlockSpec(memory_space=pltpu.SMEM)
    return pl.pallas_call(
        functools.partial(_moba_prompt_kernel, n_blocks=seq // MB_BLOCK),
        grid=(n_groups, bsz),
        in_specs=[smem, smem,
                  pl.BlockSpec((1, seq, 128), lambda g, b: (b, 0, g)),
                  pl.BlockSpec((1, 128, seq), lambda g, b: (b, g, 0)),
                  pl.BlockSpec((1, 128, seq), lambda g, b: (b, g, 0))],
        out_specs=pl.BlockSpec((1, seq, 128), lambda g, b: (b, 0, g)),
        out_shape=jax.ShapeDtypeStruct((bsz, seq, width), BF16),
        scratch_shapes=[pltpu.VMEM((2, hp, MB_BLOCK, MB_BLOCK), F32)],
        compiler_params=pltpu.CompilerParams(dimension_semantics=("parallel", "arbitrary"),
                                             vmem_limit_bytes=VMEM_SMALL),
        name="moba_prompt",
    )(_bucket_thresholds(), rel_bias.T.reshape(-1).astype(F32), mq, kt, vt)


def _moba_sample_kernel(pt_ref, q_ref, kown_ref, vown_ref, bfar_ref, blast_ref, bown_ref, ck_ref, cv_ref,
                        o_ref, buf, sem, s_all, *, n_pages, pps, layer):
    nh, hd = MB_HEADS, MB_HD
    n_tok = q_ref.shape[1]
    rows = nh * n_tok
    width = nh * hd
    ppb = MB_BLOCK // PAGE_SIZE
    n_full = n_pages // ppb
    gk = n_pages // pps
    n_groups = 2 * gk
    n_slots = buf.shape[0]
    ahead = n_slots - 1
    seq_id = pl.program_id(0)
    n_seq = pl.num_programs(0)

    def group_copies(slot, cache_ref, page_of):
        return [pltpu.make_async_copy(cache_ref.at[layer, page_of(jj)], buf.at[slot, jj], sem.at[slot])
                for jj in range(pps)]

    def start_group(seq, loc):
        slot = (seq * n_groups + loc) % n_slots

        @pl.when(loc < gk)
        def _():
            for cp in group_copies(slot, ck_ref, lambda jj: pt_ref[seq * n_pages + loc * pps + jj]):
                cp.start()

        @pl.when(loc >= gk)
        def _():
            for cp in group_copies(slot, cv_ref, lambda jj: pt_ref[seq * n_pages + (loc - gk) * pps + jj]):
                cp.start()

    def prefetch(loc):
        nxt = loc + ahead

        @pl.when(nxt < n_groups)
        def _():
            start_group(seq_id, nxt)

        @pl.when((nxt >= n_groups) & (seq_id + 1 < n_seq))
        def _():
            start_group(seq_id + 1, nxt - n_groups)

    def wait_group(loc):
        slot = (seq_id * n_groups + loc) % n_slots
        for cp in group_copies(slot, ck_ref, lambda jj: 0):
            cp.wait()
        return slot

    @pl.when(seq_id == 0)
    def _():
        for loc in range(ahead):
            start_group(seq_id, jnp.int32(loc))

    q = q_ref[0]
    rh = lax.broadcasted_iota(jnp.int32, (rows, width), 0) // n_tok
    ch = lax.broadcasted_iota(jnp.int32, (rows, width), 1) // hd
    own_head = rh == ch
    q_rep = jnp.concatenate([q] * nh, axis=0)
    qbd = jnp.where(own_head, q_rep * (hd ** -0.5), 0.0).astype(BF16)
    lane = lax.broadcasted_iota(jnp.int32, (rows, PAGE_SIZE), 1)

    lane_w = lax.broadcasted_iota(jnp.int32, (width, PAGE_SIZE), 1)

    def key_group(g, carry):
        pm, ms = carry
        slot = wait_group(g)
        prefetch(g)
        for jb in range(pps // ppb):
            ksum = None
            for pj in range(ppb):
                jj = jb * ppb + pj
                kp = buf[slot, jj].reshape(width, PAGE_SIZE)
                page = g * pps + jj
                s = _dot(qbd, kp.astype(BF16))
                s_all[page] = s
                pm = jnp.where(lane == page, jnp.max(s, axis=-1, keepdims=True), pm)
                ksum = kp if ksum is None else ksum + kp
            ms = jnp.where(lane_w == g * (pps // ppb) + jb, jnp.sum(ksum, axis=-1, keepdims=True), ms)
        return pm, ms

    pm, ms = lax.fori_loop(0, gk, key_group, (jnp.full((rows, PAGE_SIZE), -jnp.inf, F32),
                                              jnp.zeros((width, PAGE_SIZE), F32)))

    q_bd32 = jnp.where(own_head, q_rep, 0.0)
    gs = _dot_precise(q_bd32, ms * (1.0 / MB_BLOCK))
    gs = jnp.where(lane < n_full, gs, -jnp.inf)
    sel = _top_mask(gs, lane, min(MB_TOPK, n_full))
    blk_of_page = lax.broadcasted_iota(jnp.int32, (PAGE_SIZE, PAGE_SIZE), 1) // ppb
    blk_row = lax.broadcasted_iota(jnp.int32, (PAGE_SIZE, PAGE_SIZE), 0)
    sel_page = _dot(sel.astype(BF16), jnp.where(blk_row == blk_of_page, 1.0, 0.0).astype(BF16)) > 0.5
    bfar = bfar_ref[...]
    for pj in range(ppb):
        page = n_pages - ppb + pj
        s = s_all[page] + (blast_ref[:, pj * PAGE_SIZE:(pj + 1) * PAGE_SIZE] - bfar)
        s_all[page] = s
        pm = jnp.where(lane == page, jnp.max(s, axis=-1, keepdims=True), pm)

    pad = jnp.zeros((PAGE_SIZE - n_tok, width), F32)
    kown = jnp.concatenate([kown_ref[0], pad], axis=0).astype(BF16)
    vown = jnp.concatenate([vown_ref[0], pad], axis=0).astype(BF16)
    tq = lax.broadcasted_iota(jnp.int32, (rows, PAGE_SIZE), 0) % n_tok
    s_own = jnp.where(lane <= tq, _dot_nt(qbd, kown) + (bown_ref[...] - bfar), -jnp.inf)
    mx = jnp.max(jnp.maximum(jnp.where(sel_page, pm, -jnp.inf), s_own), axis=-1, keepdims=True)
    shift = jnp.where(sel_page, mx, jnp.inf)
    p_own = jnp.exp(s_own - mx)

    def value_group(g, carry):
        den, acc = carry
        slot = wait_group(gk + g)
        prefetch(gk + g)
        for jj in range(pps):
            page = g * pps + jj
            sh = jnp.min(jnp.where(lane == page, shift, jnp.inf), axis=-1, keepdims=True)
            p = jnp.exp(s_all[page] - sh)
            den = den + p
            acc = acc + _dot_nt(p.astype(BF16), buf[slot, jj].reshape(width, PAGE_SIZE).astype(BF16))
        return den, acc

    den, acc = lax.fori_loop(0, gk, value_group, (p_own, _dot(p_own.astype(BF16), vown)))
    o = jnp.where(own_head, acc / jnp.sum(den, axis=-1, keepdims=True), 0.0)
    out = o[0:n_tok]
    for h in range(1, nh):
        out = out + o[h * n_tok:(h + 1) * n_tok]
    o_ref[0] = out


def _moba_sample(mq, kown, vown, ck_t, cv_t, page_table, rel_bias, layer):
    dbs, n_tok, width = mq.shape
    n_pages = page_table.shape[1]
    past = n_pages * PAGE_SIZE
    ppb = MB_BLOCK // PAGE_SIZE
    pps = PAGES_PER_STEP
    assert past % MB_BLOCK == 0 and n_pages % pps == 0 and pps % ppb == 0 and n_pages // ppb >= MB_TOPK
    assert n_tok <= MAX_EXACT and MB_BLOCK >= MAX_DISTANCE and n_tok % 8 == 0 and n_pages <= PAGE_SIZE
    rows = MB_HEADS * n_tok
    gk = n_pages // pps
    t = jnp.arange(n_tok)
    d_last = (t[:, None] + 1) + jnp.arange(MB_BLOCK)[::-1][None, :]
    blast = rel_bias[_rel_bucket(d_last)].transpose(2, 0, 1).reshape(rows, MB_BLOCK)
    bfar = jnp.broadcast_to(rel_bias[_rel_bucket(jnp.asarray(MB_BLOCK + 1))][:, None, None],
                            (MB_HEADS, n_tok, PAGE_SIZE)).reshape(rows, PAGE_SIZE)
    d_own = t[:, None] - jnp.arange(PAGE_SIZE)[None, :]
    bown = rel_bias[_rel_bucket(d_own)].transpose(2, 0, 1).reshape(rows, PAGE_SIZE)

    tok_spec = pl.BlockSpec((1, n_tok, width), lambda b, pt: (b, 0, 0))
    full = lambda a: pl.BlockSpec(a.shape, lambda b, pt: (0,) * a.ndim)
    hbm = pl.BlockSpec(memory_space=pl.ANY)
    grid_spec = pltpu.PrefetchScalarGridSpec(
        num_scalar_prefetch=1,
        grid=(dbs,),
        in_specs=[tok_spec, tok_spec, tok_spec, full(bfar), full(blast), full(bown), hbm, hbm],
        out_specs=tok_spec,
        scratch_shapes=[pltpu.VMEM((RING_SLOTS, pps, MB_HEADS, MB_HD, PAGE_SIZE), F32),
                        pltpu.SemaphoreType.DMA((RING_SLOTS,)),
                        pltpu.VMEM((n_pages, rows, PAGE_SIZE), F32)],
    )
    return pl.pallas_call(
        functools.partial(_moba_sample_kernel, n_pages=n_pages, pps=pps, layer=layer),
        grid_spec=grid_spec,
        out_shape=jax.ShapeDtypeStruct((dbs, n_tok, width), F32),
        compiler_params=pltpu.CompilerParams(dimension_semantics=("arbitrary",),
                                             vmem_limit_bytes=VMEM_SMALL),
        name="moba_sample",
    )(page_table.reshape(-1), mq, kown, vown, bfar, blast, bown, ck_t, cv_t)


def kernel(x_prompt, x_sample, state_hgrn, cache_k, cache_v, page_table, ffn1_norm, ffn1_w_gate, ffn1_w_up,
           ffn1_w_down, mix_norm, w_in, hg_lb, hg_norm, rel_bias, w_branch_a, w_branch_b, w_out, ffn2_norm,
           ffn2_w_gate, ffn2_w_up, ffn2_w_down, final_norm):
    depth = w_in.shape[0]
    assert depth == 1, "the final norm is fused into the (single) layer's last kernel"
    bsz, seq, d = x_prompt.shape
    dbs, n_tok, _ = x_sample.shape
    nf, nv, mw = HG_HEADS * HG_DK, HG_HEADS * HG_DV, MB_HEADS * MB_HD
    n_hg = 2 * nf + 2 * nv
    assert w_in.shape[2] == n_hg + 3 * mw + 2 * d

    lbs = jnp.cumsum(jax.nn.softmax(hg_lb.astype(F32), axis=0), axis=0)
    ck_t = jnp.swapaxes(cache_k, 3, 4)
    cv_t = jnp.swapaxes(cache_v, 3, 4)
    final_g = final_norm.reshape(1, d)

    xp = x_prompt
    xs = x_sample.reshape(1, dbs * n_tok, d)
    sp_l, ss_l, kp_l, vp_l, ks_l, vs_l = [], [], [], [], [], []
    yp = ys = None
    for l in range(depth):
        w = w_in[l]
        k0 = n_hg + mw
        ffn_in_w = (ffn1_norm[l].reshape(1, d), ffn1_w_gate[l].astype(BF16), ffn1_w_up[l].astype(BF16),
                    ffn1_w_down[l].astype(BF16), mix_norm[l].reshape(1, d), w[:, :k0].astype(BF16),
                    w[:, k0:k0 + mw].T.astype(BF16), w[:, k0 + mw:k0 + 2 * mw].T.astype(BF16),
                    w[:, k0 + 2 * mw:].astype(BF16))
        merge_w = (w_branch_a[l].astype(BF16), w_branch_b[l].astype(BF16), w_out[l].astype(BF16),
                   ffn2_norm[l].reshape(1, d), ffn2_w_gate[l].astype(BF16), ffn2_w_up[l].astype(BF16),
                   ffn2_w_down[l].astype(BF16), final_g)
        lb = lbs[l].reshape(1, nf)
        ng = hg_norm[l].reshape(1, HG_DV)

        x1p, zhp, mqp, ktp, vtp, gabp = _ffn_in(xp, *ffn_in_w)
        ya_p, st_p = _hgrn(zhp, jnp.zeros((bsz, HG_HEADS, HG_DK, HG_DV), F32), lb, ng, seq)
        yb_p = _moba_prompt(mqp, ktp, vtp, rel_bias)
        yp = _merge_ffn(x1p, ya_p, yb_p, gabp, *merge_w)

        x1s, zhs, mqs, kts, vts, gabs = _ffn_in(xs, *ffn_in_w)
        pad_tok = max(16, n_tok)
        zhs_b = jnp.pad(zhs.reshape(dbs, n_tok, n_hg), ((0, 0), (0, pad_tok - n_tok), (0, 0)))
        ya_s, st_s = _hgrn(zhs_b, state_hgrn[l], lb, ng, n_tok)
        ya_s = ya_s[:, :n_tok].reshape(1, dbs * n_tok, nv)
        kown = kts[0].T.reshape(dbs, n_tok, mw)
        vown = vts[0].T.reshape(dbs, n_tok, mw)
        yb_s = _moba_sample(mqs.reshape(dbs, n_tok, mw), kown, vown, ck_t, cv_t, page_table, rel_bias, l)
        ys = _merge_ffn(x1s, ya_s, yb_s.reshape(1, dbs * n_tok, mw), gabs, *merge_w)

        sp_l.append(st_p)
        ss_l.append(st_s)
        kp_l.append(jnp.swapaxes(ktp.reshape(bsz, MB_HEADS, MB_HD, seq), 2, 3))
        vp_l.append(jnp.swapaxes(vtp.reshape(bsz, MB_HEADS, MB_HD, seq), 2, 3))
        ks_l.append(kown.reshape(dbs, n_tok, MB_HEADS, MB_HD).transpose(0, 2, 1, 3))
        vs_l.append(vown.reshape(dbs, n_tok, MB_HEADS, MB_HD).transpose(0, 2, 1, 3))
    return (yp, ys.reshape(dbs, n_tok, d), jnp.stack(sp_l), jnp.stack(ss_l), jnp.stack(kp_l), jnp.stack(vp_l),
            jnp.stack(ks_l), jnp.stack(vs_l))
```

```python
import functools
import math

import numpy as np
import jax
import jax.numpy as jnp
from jax import lax
from jax.experimental import pallas as pl
from jax.experimental.pallas import tpu as pltpu

F32 = jnp.float32
BF16 = jnp.bfloat16

EPS = 1e-6
NEG = -1e30
HG_HEADS = 4
HG_DK = 128
HG_DV = 128
MB_HEADS = 8
MB_HD = 64
MB_BLOCK = 256
MB_TOPK = 3
PAGE_SIZE = 128
N_BUCKETS = 32
MAX_EXACT = N_BUCKETS // 2
MAX_DISTANCE = 128

V7X_VMEM_BYTES = 64 * 2**20
MXU_DEPTH = 256
VMEM_BIG = V7X_VMEM_BYTES - 8 * 2**20
VMEM_SMALL = 40 * 2**20

ROW_TILE = 256
ROW_TILE_MERGE = 512
HG_CHUNK = 128
HG_STEP = 512
HG_UNROLL = 2
HG_SHORT_SEQS = 8
HG_LONG_SEQS = 2
MB_QROWS = 256
PAGES_PER_STEP = 16
RING_SLOTS = 4

_NT = (((1,), (1,)), ((), ()))


def _dot(a, b):
    return jnp.dot(a, b, preferred_element_type=F32)


def _dot_nt(a, b):
    return lax.dot_general(a, b, _NT, preferred_element_type=F32)


def _split2(a):
    hi = a.astype(BF16)
    lo = (a - hi.astype(F32)).astype(BF16)
    return hi, lo


def _dot_precise(a, b):
    a1, a2 = _split2(a)
    b1, b2 = _split2(b)
    if 3 * a.shape[-1] <= MXU_DEPTH:
        return _dot(jnp.concatenate([a1, a1, a2], axis=-1), jnp.concatenate([b1, b2, b1], axis=0))
    return _dot(a1, b1) + _dot(a1, b2) + _dot(a2, b1)


def _rms(x, g):
    return x * lax.rsqrt(jnp.mean(x * x, axis=-1, keepdims=True) + EPS) * g


def _swiglu(h, wg_ref, wu_ref, wd_ref, ff_chunk):
    d_ff = wg_ref.shape[1]
    acc = None
    for c0 in range(0, d_ff, ff_chunk):
        c1 = min(c0 + ff_chunk, d_ff)
        a = _dot(h, wg_ref[:, c0:c1])
        u = _dot(h, wu_ref[:, c0:c1])
        act = (a * jax.nn.sigmoid(a) * u).astype(BF16)
        part = _dot(act, wd_ref[c0:c1, :])
        acc = part if acc is None else acc + part
    return acc


def _ff_chunk(d_ff):
    return -(-d_ff // (2 * MXU_DEPTH)) * MXU_DEPTH if d_ff % MXU_DEPTH == 0 else d_ff


def _const_spec(shape):
    nd = len(shape)
    return pl.BlockSpec(shape, lambda *_: (0,) * nd, pipeline_mode=pl.Buffered(1))


def _ffn_in_kernel(x_ref, n1_ref, wg_ref, wu_ref, wd_ref, nm_ref, wz_ref, wkt_ref, wvt_ref, wgab_ref,
                   x1_ref, zh_ref, mq_ref, kt_ref, vt_ref, gab_ref, *, ff_chunk, n_hg):
    x = x_ref[0]
    h = _rms(x, n1_ref[...]).astype(BF16)
    x1 = x + 0.5 * _swiglu(h, wg_ref, wu_ref, wd_ref, ff_chunk)
    x1_ref[0] = x1
    h2 = _rms(x1, nm_ref[...]).astype(BF16)
    z = _dot(h2, wz_ref[...])
    zh_ref[0] = z[:, :n_hg]
    mq_ref[0] = z[:, n_hg:]
    kt_ref[0] = _dot_nt(wkt_ref[...], h2)
    vt_ref[0] = _dot_nt(wvt_ref[...], h2)
    gab_ref[0] = _dot(h2, wgab_ref[...]).astype(BF16)


def _ffn_in(x, n1, wg, wu, wd, nm, wz, wkt, wvt, wgab):
    bsz, seq, d = x.shape
    tm = min(ROW_TILE, seq)
    assert seq % tm == 0
    n_z = wz.shape[1]
    n_hg = 2 * HG_HEADS * HG_DK + 2 * HG_HEADS * HG_DV
    n_kv = wkt.shape[0]
    n_gab = wgab.shape[1]
    row = lambda w: pl.BlockSpec((1, tm, w), lambda b, i: (b, i, 0))
    col = pl.BlockSpec((1, n_kv, tm), lambda b, i: (b, 0, i))
    outs = pl.pallas_call(
        functools.partial(_ffn_in_kernel, ff_chunk=_ff_chunk(wg.shape[1]), n_hg=n_hg),
        grid=(bsz, seq // tm),
        in_specs=[row(d), _const_spec(n1.shape), _const_spec(wg.shape), _const_spec(wu.shape),
                  _const_spec(wd.shape), _const_spec(nm.shape), _const_spec(wz.shape),
                  _const_spec(wkt.shape), _const_spec(wvt.shape), _const_spec(wgab.shape)],
        out_specs=[row(d), row(n_hg), row(n_z - n_hg), col, col, row(n_gab)],
        out_shape=[jax.ShapeDtypeStruct((bsz, seq, d), F32),
                   jax.ShapeDtypeStruct((bsz, seq, n_hg), F32),
                   jax.ShapeDtypeStruct((bsz, seq, n_z - n_hg), F32),
                   jax.ShapeDtypeStruct((bsz, n_kv, seq), F32),
                   jax.ShapeDtypeStruct((bsz, n_kv, seq), F32),
                   jax.ShapeDtypeStruct((bsz, seq, n_gab), BF16)],
        compiler_params=pltpu.CompilerParams(dimension_semantics=("parallel", "parallel"),
                                             vmem_limit_bytes=VMEM_BIG),
        name="ffn_in",
    )(x, n1, wg, wu, wd, nm, wz, wkt, wvt, wgab)
    return outs


def _merge_ffn_kernel(x1_ref, ya_ref, yb_ref, gab_ref, wa_ref, wb_ref, wo_ref, n2_ref,
                      wg_ref, wu_ref, wd_ref, nf_ref, y_ref, *, ff_chunk):
    d = x1_ref.shape[-1]
    gab = gab_ref[0].astype(F32)
    ma = _dot(ya_ref[0].astype(BF16), wa_ref[...])
    mb = _dot(yb_ref[0].astype(BF16), wb_ref[...])
    m = jax.nn.sigmoid(gab[:, :d]) * ma + jax.nn.sigmoid(gab[:, d:]) * mb
    x2 = x1_ref[0] + _dot(m.astype(BF16), wo_ref[...])
    h = _rms(x2, n2_ref[...]).astype(BF16)
    x3 = x2 + 0.5 * _swiglu(h, wg_ref, wu_ref, wd_ref, ff_chunk)
    y_ref[0] = _rms(x3, nf_ref[...])


def _merge_ffn(x1, ya, yb, gab, wa, wb, wo, n2, wg, wu, wd, nf):
    bsz, seq, d = x1.shape
    tm = min(ROW_TILE_MERGE, seq)
    assert seq % tm == 0
    row = lambda w: pl.BlockSpec((1, tm, w), lambda b, i: (b, i, 0))
    return pl.pallas_call(
        functools.partial(_merge_ffn_kernel, ff_chunk=_ff_chunk(wg.shape[1])),
        grid=(bsz, seq // tm),
        in_specs=[row(d), row(ya.shape[-1]), row(yb.shape[-1]), row(gab.shape[-1]),
                  _const_spec(wa.shape), _const_spec(wb.shape), _const_spec(wo.shape), _const_spec(n2.shape),
                  _const_spec(wg.shape), _const_spec(wu.shape), _const_spec(wd.shape), _const_spec(nf.shape)],
        out_specs=row(d),
        out_shape=jax.ShapeDtypeStruct((bsz, seq, d), F32),
        compiler_params=pltpu.CompilerParams(dimension_semantics=("parallel", "parallel"),
                                             vmem_limit_bytes=VMEM_BIG),
        name="merge_ffn",
    )(x1, ya, yb, gab, wa, wb, wo, n2, wg, wu, wd, nf)


def _hgrn_levels(chunk):
    levels, m = [], 1
    while m < chunk:
        levels.append(m)
        m *= 2
    return levels


HG_SPLIT = 2


def _hgrn_decay_matrix(chunk):
    t = np.arange(chunk)[:, None]
    u = np.arange(chunk)[None, :]
    mats = [u <= t]
    for m in _hgrn_levels(chunk):
        mid_q = t & ~(m - 1)
        q_side = ((t & m) != 0) & (u >= mid_q) & (u <= t)
        mid_k = (t & ~(2 * m - 1)) + m
        k_side = ((t & m) == 0) & (u > t) & (u < mid_k)
        mats.append(q_side | k_side)
    return np.tile(np.concatenate(mats, axis=0).astype(np.float32), (1, HG_SPLIT))


def _hgrn_kernel(zh_ref, s0_ref, lb_ref, ng_ref, dm_ref, ya_ref, sout_ref, st_ref,
                 *, chunk, n_inner, valid_len, n_seq):
    nh, dk, dv = HG_HEADS, HG_DK, HG_DV
    nf = nh * dk
    j = pl.program_id(1)

    @pl.when(j == 0)
    def _():
        for s in range(n_seq):
            for h in range(nh):
                st_ref[s, h] = s0_ref[s, h].T

    lb = lb_ref[...]
    ng = ng_ref[...]
    dmat = dm_ref[...]
    ti = lax.broadcasted_iota(jnp.int32, (chunk, chunk), 0)
    si = lax.broadcasted_iota(jnp.int32, (chunk, chunk), 1)
    xs = jnp.where(ti > si, jnp.bitwise_xor(ti, si), 0)
    levels = _hgrn_levels(chunk)

    def one_chunk(ci, carry):
        for s in range(n_seq):
            seq_chunk(s, pl.multiple_of(ci * chunk, chunk))
        return carry

    def seq_chunk(s, r0):
        zh = zh_ref[s, pl.ds(r0, chunk), :]
        hq, hf = zh[:, :nf], zh[:, nf:2 * nf]
        hv, hg = zh[:, 2 * nf:2 * nf + nh * dv], zh[:, 2 * nf + nh * dv:]
        f = lb + (1.0 - lb) * jax.nn.sigmoid(hf)
        logf = jnp.log2(f)
        kk = 1.0 - f
        if valid_len < chunk:
            live = lax.broadcasted_iota(jnp.int32, (chunk, 1), 0) < valid_len
            logf = jnp.where(live, logf, 0.0)
            kk = jnp.where(live, kk, 0.0)
        qf = hq * jax.nn.sigmoid(hq)
        pieces, rest = [], logf
        for _ in range(HG_SPLIT):
            piece = rest.astype(BF16)
            pieces.append(piece)
            rest = rest - piece.astype(F32)
        dec = _dot(dmat, jnp.concatenate(pieces, axis=0))
        outs = []
        for h in range(nh):
            ks = slice(h * dk, (h + 1) * dk)
            vs = slice(h * dv, (h + 1) * dv)
            q_h, k_h, v_h = qf[:, ks], kk[:, ks], hv[:, vs]
            b = dec[0:chunk, ks]
            after = b[chunk - 1:chunk, :] - b
            st = st_ref[s, h]
            o = _dot_nt((q_h * jnp.exp2(b)).astype(BF16), st.astype(BF16))
            att = None
            for li, m in enumerate(levels):
                w = jnp.exp2(dec[(1 + li) * chunk:(2 + li) * chunk, ks])
                a_l = _dot_nt((q_h * w).astype(BF16), (k_h * w).astype(BF16))
                att = a_l if att is None else jnp.where(xs >= m, a_l, att)
            diag = jnp.sum(q_h * k_h, axis=-1, keepdims=True)
            att = jnp.where(xs >= 1, att, 0.0)
            att = jnp.where(ti == si, diag, att)
            o = o + _dot(att.astype(BF16), v_h.astype(BF16))
            kd = (k_h * jnp.exp2(after)).astype(BF16)
            st_ref[s, h] = jnp.exp2(b[chunk - 1:chunk, :]) * st + _dot(v_h.T.astype(BF16), kd)
            o = o * lax.rsqrt(jnp.mean(o * o, axis=-1, keepdims=True) + EPS) * ng
            g_h = hg[:, vs]
            outs.append((o * (g_h * jax.nn.sigmoid(g_h))).astype(BF16))
        ya_ref[s, pl.ds(r0, chunk), :] = jnp.concatenate(outs, axis=-1)

    lax.fori_loop(0, n_inner, one_chunk, 0, unroll=min(HG_UNROLL, n_inner))

    @pl.when(j == pl.num_programs(1) - 1)
    def _():
        for s in range(n_seq):
            for h in range(nh):
                sout_ref[s, h] = st_ref[s, h].T


def _hgrn(zh, s0, lb, ng, valid_len):
    bsz, seq, width = zh.shape
    chunk = min(HG_CHUNK, seq)
    step = min(HG_STEP, seq)
    assert seq % step == 0 and step % chunk == 0 and chunk % 16 == 0
    dmat = jnp.asarray(_hgrn_decay_matrix(chunk), BF16)
    nv = HG_HEADS * HG_DV
    n_seq = math.gcd(bsz, HG_SHORT_SEQS if seq == chunk else HG_LONG_SEQS)
    st_spec = pl.BlockSpec((n_seq, HG_HEADS, HG_DK, HG_DV), lambda b, j: (b, 0, 0, 0))
    return pl.pallas_call(
        functools.partial(_hgrn_kernel, chunk=chunk, n_inner=step // chunk, valid_len=min(valid_len, chunk),
                          n_seq=n_seq),
        grid=(bsz // n_seq, seq // step),
        in_specs=[pl.BlockSpec((n_seq, step, width), lambda b, j: (b, j, 0)), st_spec,
                  _const_spec(lb.shape), _const_spec(ng.shape), _const_spec(dmat.shape)],
        out_specs=[pl.BlockSpec((n_seq, step, nv), lambda b, j: (b, j, 0)), st_spec],
        out_shape=[jax.ShapeDtypeStruct((bsz, seq, nv), BF16),
                   jax.ShapeDtypeStruct((bsz, HG_HEADS, HG_DK, HG_DV), F32)],
        scratch_shapes=[pltpu.VMEM((n_seq, HG_HEADS, HG_DV, HG_DK), F32)],
        compiler_params=pltpu.CompilerParams(dimension_semantics=("parallel", "arbitrary"),
                                             vmem_limit_bytes=VMEM_SMALL),
        name="hgrn",
    )(zh, s0, lb, ng, dmat)


def _rel_bucket(dist):
    n = jnp.maximum(dist, 0)
    nf = jnp.maximum(n, 1).astype(F32)
    large = MAX_EXACT + (jnp.log(nf / MAX_EXACT) / math.log(MAX_DISTANCE / MAX_EXACT)
                         * (N_BUCKETS - MAX_EXACT)).astype(jnp.int32)
    large = jnp.minimum(large, N_BUCKETS - 1)
    return jnp.where(n < MAX_EXACT, n, large)


def _top_mask(gs, lane, n_sel):
    mask = jnp.zeros(gs.shape, F32)
    lane_f = lane.astype(F32)
    cur = gs
    for _ in range(n_sel):
        best = jnp.max(cur, axis=-1, keepdims=True)
        idx = jnp.min(jnp.where(cur == best, lane_f, float(gs.shape[-1])), axis=-1, keepdims=True)
        hit = lane_f == idx
        mask = jnp.where(hit, 1.0, mask)
        cur = jnp.where(hit, -jnp.inf, cur)
    return mask


def _moba_prompt_kernel(th_ref, rb_ref, q_ref, kt_ref, vt_ref, o_ref, bias_sc, *, n_blocks):
    blk, hd = MB_BLOCK, MB_HD
    heads = q_ref.shape[-1] // hd
    seq = q_ref.shape[1]
    log2e = 1.0 / math.log(2.0)
    scale = hd ** -0.5 * log2e
    grp = pl.program_id(0)
    ti = lax.broadcasted_iota(jnp.int32, (blk, blk), 0)
    si = lax.broadcasted_iota(jnp.int32, (blk, blk), 1)
    causal = ti >= si

    @pl.when(pl.program_id(1) == 0)
    def _():
        for delta in range(2):
            d = jnp.maximum(delta * blk + ti - si, 0)
            for hl in range(heads):
                base = (grp * heads + hl) * N_BUCKETS
                far = rb_ref[base + N_BUCKETS - 1]
                val = jnp.full((blk, blk), far, F32)
                for bk in range(N_BUCKETS - 2, -1, -1):
                    val = jnp.where(d < th_ref[bk + 1], rb_ref[base + bk], val)
                bias_sc[delta, hl] = (val - far) * log2e

    row = lax.broadcasted_iota(jnp.int32, (hd, seq), 0)
    col_blk = lax.broadcasted_iota(jnp.int32, (hd, seq), 1) // blk
    indicator = jnp.where(row == col_blk, 1.0, 0.0).astype(BF16)
    ones_row = jnp.where(row == 0, 1.0, 0.0).astype(BF16)

    nb8 = -(-n_blocks // 8) * 8
    blk_idx = lax.broadcasted_iota(jnp.int32, (nb8, seq), 0)
    q_blk = lax.broadcasted_iota(jnp.int32, (nb8, seq), 1) // blk
    fully_past = blk_idx < q_blk

    q_all = q_ref[0]
    qs, ktx, vtx = [], [], []
    for hl in range(heads):
        q = q_all[:, hl * hd:(hl + 1) * hd]
        kt = kt_ref[0, hl * hd:(hl + 1) * hd, :]
        ktx.append(jnp.concatenate([kt.astype(BF16), indicator], axis=0))
        vtx.append(jnp.concatenate([vt_ref[0, hl * hd:(hl + 1) * hd, :].astype(BF16), ones_row], axis=0))
        if n_blocks > 1:
            lane_k = lax.broadcasted_iota(jnp.int32, (hd, 128), 1)
            kmean = jnp.zeros((hd, 128), F32)
            for n in range(n_blocks):
                ksum = jnp.sum(kt[:, n * blk:(n + 1) * blk], axis=-1, keepdims=True)
                kmean = jnp.where(lane_k == n, ksum * (1.0 / blk), kmean)
            gate = _dot_precise(q, kmean).T[:nb8]
            gate = jnp.where(fully_past, gate, -jnp.inf)
            rank = jnp.zeros((nb8, seq), F32)
            for r in range(1, nb8):
                other, other_idx = pltpu.roll(gate, r, 0), pltpu.roll(blk_idx, r, 0)
                beats = (other > gate) | ((other == gate) & (other_idx < blk_idx))
                rank = rank + jnp.where(beats, 1.0, 0.0)
            keep = (fully_past & (rank < MB_TOPK)) | (blk_idx >= q_blk)
            pen_t = jnp.concatenate([jnp.where(keep, 0.0, NEG), jnp.zeros((128 - nb8, seq), F32)], axis=0)
            pen = pen_t.T[:, :hd].astype(BF16)
        else:
            pen = jnp.zeros((seq, hd), BF16)
        qs.append(jnp.concatenate([(q * scale).astype(BF16), pen], axis=-1))

    qr = MB_QROWS
    for i, sub in [(i, sub) for i in range(n_blocks) for sub in range(blk // qr)]:
        rows = slice(i * blk + sub * qr, i * blk + (sub + 1) * qr)
        in_blk = slice(sub * qr, (sub + 1) * qr)
        outs = []
        for hl in range(heads):
            s_all = _dot(qs[hl][rows], ktx[hl][:, :(i + 1) * blk])
            tiles = []
            for jb in range(i + 1):
                s = s_all[:, jb * blk:(jb + 1) * blk]
                if jb == i:
                    s = jnp.where(causal[in_blk], s + bias_sc[0, hl, in_blk, :], NEG)
                elif jb == i - 1:
                    s = s + bias_sc[1, hl, in_blk, :]
                tiles.append(s)
            mx = tiles[0]
            for s in tiles[1:]:
                mx = jnp.maximum(mx, s)
            mx = jnp.max(mx, axis=-1, keepdims=True)
            p = jnp.concatenate([jnp.exp2(s - mx).astype(BF16) for s in tiles], axis=-1)
            acc = _dot_nt(p, vtx[hl][:, :(i + 1) * blk])
            outs.append((acc[:, :hd] / acc[:, hd:hd + 1]).astype(BF16))
        o_ref[0, rows, :] = jnp.concatenate(outs, axis=-1)


def _bucket_thresholds():
    buckets = _rel_bucket(jnp.arange(MAX_DISTANCE + 1))
    return jnp.sum(buckets[None, :] < jnp.arange(N_BUCKETS + 1)[:, None], axis=1).astype(jnp.int32)


def _moba_prompt(mq, kt, vt, rel_bias):
    bsz, seq, width = mq.shape
    assert seq % MB_BLOCK == 0 and MB_BLOCK >= MAX_DISTANCE and seq // MB_BLOCK <= MB_HD
    hp = 128 // MB_HD
    n_groups = width // 128
    smem = pl.BlockSpec(memory_space=pltpu.SMEM)
    return pl.pallas_call(
        functools.partial(_moba_prompt_kernel, n_blocks=seq // MB_BLOCK),
        grid=(n_groups, bsz),
        in_specs=[smem, smem,
                  pl.BlockSpec((1, seq, 128), lambda g, b: (b, 0, g)),
                  pl.BlockSpec((1, 128, seq), lambda g, b: (b, g, 0)),
                  pl.BlockSpec((1, 128, seq), lambda g, b: (b, g, 0))],
        out_specs=pl.BlockSpec((1, seq, 128), lambda g, b: (b, 0, g)),
        out_shape=jax.ShapeDtypeStruct((bsz, seq, width), BF16),
        scratch_shapes=[pltpu.VMEM((2, hp, MB_BLOCK, MB_BLOCK), F32)],
        compiler_params=pltpu.CompilerParams(dimension_semantics=("parallel", "arbitrary"),
                                             vmem_limit_bytes=VMEM_SMALL),
        name="moba_prompt",
    )(_bucket_thresholds(), rel_bias.T.reshape(-1).astype(F32), mq, kt, vt)


def _moba_sample_kernel(pt_ref, q_ref, kown_ref, vown_ref, bfar_ref, blast_ref, bown_ref, ck_ref, cv_ref,
                        o_ref, buf, sem, s_all, *, n_pages, pps, layer):
    nh, hd = MB_HEADS, MB_HD
    n_tok = q_ref.shape[1]
    rows = nh * n_tok
    width = nh * hd
    ppb = MB_BLOCK // PAGE_SIZE
    n_full = n_pages // ppb
    gk = n_pages // pps
    n_groups = 2 * gk
    n_slots = buf.shape[0]
    ahead = n_slots - 1
    seq_id = pl.program_id(0)
    n_seq = pl.num_programs(0)

    def group_copies(slot, cache_ref, page_of):
        return [pltpu.make_async_copy(cache_ref.at[layer, page_of(jj)], buf.at[slot, jj], sem.at[slot])
                for jj in range(pps)]

    def start_group(seq, loc):
        slot = (seq * n_groups + loc) % n_slots

        @pl.when(loc < gk)
        def _():
            for cp in group_copies(slot, ck_ref, lambda jj: pt_ref[seq * n_pages + loc * pps + jj]):
                cp.start()

        @pl.when(loc >= gk)
        def _():
            for cp in group_copies(slot, cv_ref, lambda jj: pt_ref[seq * n_pages + (loc - gk) * pps + jj]):
                cp.start()

    def prefetch(loc):
        nxt = loc + ahead

        @pl.when(nxt < n_groups)
        def _():
            start_group(seq_id, nxt)

        @pl.when((nxt >= n_groups) & (seq_id + 1 < n_seq))
        def _():
            start_group(seq_id + 1, nxt - n_groups)

    def wait_group(loc):
        slot = (seq_id * n_groups + loc) % n_slots
        for cp in group_copies(slot, ck_ref, lambda jj: 0):
            cp.wait()
        return slot

    @pl.when(seq_id == 0)
    def _():
        for loc in range(ahead):
            start_group(seq_id, jnp.int32(loc))

    q = q_ref[0]
    rh = lax.broadcasted_iota(jnp.int32, (rows, width), 0) // n_tok
    ch = lax.broadcasted_iota(jnp.int32, (rows, width), 1) // hd
    own_head = rh == ch
    q_rep = jnp.concatenate([q] * nh, axis=0)
    qbd = jnp.where(own_head, q_rep * (hd ** -0.5), 0.0).astype(BF16)
    lane = lax.broadcasted_iota(jnp.int32, (rows, PAGE_SIZE), 1)

    lane_w = lax.broadcasted_iota(jnp.int32, (width, PAGE_SIZE), 1)

    def key_group(g, carry):
        pm, ms = carry
        slot = wait_group(g)
        prefetch(g)
        for jb in range(pps // ppb):
            blk_n = g * (pps // ppb) + jb
            kps = [buf[slot, jb * ppb + pj].reshape(width, PAGE_SIZE) for pj in range(ppb)]
            s_blk = _dot(qbd, jnp.concatenate([kp.astype(BF16) for kp in kps], axis=-1))
            s_max, ksum = None, None
            for pj, kp in enumerate(kps):
                s = s_blk[:, pj * PAGE_SIZE:(pj + 1) * PAGE_SIZE]
                s_all[g * pps + jb * ppb + pj] = s
                s_max = s if s_max is None else jnp.maximum(s_max, s)
                ksum = kp if ksum is None else ksum + kp
            pm = jnp.where(lane == blk_n, jnp.max(s_max, axis=-1, keepdims=True), pm)
            ms = jnp.where(lane_w == blk_n, jnp.sum(ksum, axis=-1, keepdims=True), ms)
        return pm, ms

    pm, ms = lax.fori_loop(0, gk, key_group, (jnp.full((rows, PAGE_SIZE), -jnp.inf, F32),
                                              jnp.zeros((width, PAGE_SIZE), F32)))

    q_bd32 = jnp.where(own_head, q_rep, 0.0)
    gs = _dot_precise(q_bd32, ms * (1.0 / MB_BLOCK))
    gs = jnp.where(lane < n_full, gs, -jnp.inf)
    sel = _top_mask(gs, lane, min(MB_TOPK, n_full))
    sel = sel > 0.5
    bfar = bfar_ref[...]
    s_max = None
    for pj in range(ppb):
        page = n_pages - ppb + pj
        s = s_all[page] + (blast_ref[:, pj * PAGE_SIZE:(pj + 1) * PAGE_SIZE] - bfar)
        s_all[page] = s
        s_max = s if s_max is None else jnp.maximum(s_max, s)
    pm = jnp.where(lane == n_full - 1, jnp.max(s_max, axis=-1, keepdims=True), pm)

    pad = jnp.zeros((PAGE_SIZE - n_tok, width), F32)
    kown = jnp.concatenate([kown_ref[0], pad], axis=0).astype(BF16)
    vown = jnp.concatenate([vown_ref[0], pad], axis=0).astype(BF16)
    tq = lax.broadcasted_iota(jnp.int32, (rows, PAGE_SIZE), 0) % n_tok
    s_own = jnp.where(lane <= tq, _dot_nt(qbd, kown) + (bown_ref[...] - bfar), -jnp.inf)
    mx = jnp.max(jnp.maximum(jnp.where(sel, pm, -jnp.inf), s_own), axis=-1, keepdims=True)
    shift = jnp.where(sel, mx, jnp.inf)
    p_own = jnp.exp(s_own - mx)

    def value_group(g, carry):
        den, acc = carry
        slot = wait_group(gk + g)
        prefetch(gk + g)
        probs = []
        for jb in range(pps // ppb):
            blk_n = g * (pps // ppb) + jb
            sh = jnp.min(jnp.where(lane == blk_n, shift, jnp.inf), axis=-1, keepdims=True)
            for pj in range(ppb):
                p = jnp.exp(s_all[g * pps + jb * ppb + pj] - sh)
                den = den + p
                probs.append(p.astype(BF16))
        v_grp = jnp.concatenate([buf[slot, jj].reshape(width, PAGE_SIZE).astype(BF16) for jj in range(pps)], axis=-1)
        acc = acc + _dot_nt(jnp.concatenate(probs, axis=-1), v_grp)
        return den, acc

    den, acc = lax.fori_loop(0, gk, value_group, (p_own, _dot(p_own.astype(BF16), vown)))
    o = jnp.where(own_head, acc / jnp.sum(den, axis=-1, keepdims=True), 0.0)
    out = o[0:n_tok]
    for h in range(1, nh):
        out = out + o[h * n_tok:(h + 1) * n_tok]
    o_ref[0] = out


def _moba_sample(mq, kown, vown, ck_t, cv_t, page_table, rel_bias, layer):
    dbs, n_tok, width = mq.shape
    n_pages = page_table.shape[1]
    past = n_pages * PAGE_SIZE
    ppb = MB_BLOCK // PAGE_SIZE
    pps = PAGES_PER_STEP
    assert past % MB_BLOCK == 0 and n_pages % pps == 0 and pps % ppb == 0 and n_pages // ppb >= MB_TOPK
    assert n_tok <= MAX_EXACT and MB_BLOCK >= MAX_DISTANCE and n_tok % 8 == 0 and n_pages <= PAGE_SIZE
    rows = MB_HEADS * n_tok
    gk = n_pages // pps
    t = jnp.arange(n_tok)
    d_last = (t[:, None] + 1) + jnp.arange(MB_BLOCK)[::-1][None, :]
    blast = rel_bias[_rel_bucket(d_last)].transpose(2, 0, 1).reshape(rows, MB_BLOCK)
    bfar = jnp.broadcast_to(rel_bias[_rel_bucket(jnp.asarray(MB_BLOCK + 1))][:, None, None],
                            (MB_HEADS, n_tok, PAGE_SIZE)).reshape(rows, PAGE_SIZE)
    d_own = t[:, None] - jnp.arange(PAGE_SIZE)[None, :]
    bown = rel_bias[_rel_bucket(d_own)].transpose(2, 0, 1).reshape(rows, PAGE_SIZE)

    tok_spec = pl.BlockSpec((1, n_tok, width), lambda b, pt: (b, 0, 0))
    full = lambda a: pl.BlockSpec(a.shape, lambda b, pt: (0,) * a.ndim)
    hbm = pl.BlockSpec(memory_space=pl.ANY)
    grid_spec = pltpu.PrefetchScalarGridSpec(
        num_scalar_prefetch=1,
        grid=(dbs,),
        in_specs=[tok_spec, tok_spec, tok_spec, full(bfar), full(blast), full(bown), hbm, hbm],
        out_specs=tok_spec,
        scratch_shapes=[pltpu.VMEM((RING_SLOTS, pps, MB_HEADS, MB_HD, PAGE_SIZE), F32),
                        pltpu.SemaphoreType.DMA((RING_SLOTS,)),
                        pltpu.VMEM((n_pages, rows, PAGE_SIZE), F32)],
    )
    return pl.pallas_call(
        functools.partial(_moba_sample_kernel, n_pages=n_pages, pps=pps, layer=layer),
        grid_spec=grid_spec,
        out_shape=jax.ShapeDtypeStruct((dbs, n_tok, width), F32),
        compiler_params=pltpu.CompilerParams(dimension_semantics=("arbitrary",),
                                             vmem_limit_bytes=VMEM_SMALL),
        name="moba_sample",
    )(page_table.reshape(-1), mq, kown, vown, bfar, blast, bown, ck_t, cv_t)


def kernel(x_prompt, x_sample, state_hgrn, cache_k, cache_v, page_table, ffn1_norm, ffn1_w_gate, ffn1_w_up,
           ffn1_w_down, mix_norm, w_in, hg_lb, hg_norm, rel_bias, w_branch_a, w_branch_b, w_out, ffn2_norm,
           ffn2_w_gate, ffn2_w_up, ffn2_w_down, final_norm):
    depth = w_in.shape[0]
    assert depth == 1, "the final norm is fused into the (single) layer's last kernel"
    bsz, seq, d = x_prompt.shape
    dbs, n_tok, _ = x_sample.shape
    nf, nv, mw = HG_HEADS * HG_DK, HG_HEADS * HG_DV, MB_HEADS * MB_HD
    n_hg = 2 * nf + 2 * nv
    assert w_in.shape[2] == n_hg + 3 * mw + 2 * d

    lbs = jnp.cumsum(jax.nn.softmax(hg_lb.astype(F32), axis=0), axis=0)
    ck_t = jnp.swapaxes(cache_k, 3, 4)
    cv_t = jnp.swapaxes(cache_v, 3, 4)
    final_g = final_norm.reshape(1, d)

    xp = x_prompt
    xs = x_sample.reshape(1, dbs * n_tok, d)
    sp_l, ss_l, kp_l, vp_l, ks_l, vs_l = [], [], [], [], [], []
    yp = ys = None
    for l in range(depth):
        w = w_in[l]
        k0 = n_hg + mw
        ffn_in_w = (ffn1_norm[l].reshape(1, d), ffn1_w_gate[l].astype(BF16), ffn1_w_up[l].astype(BF16),
                    ffn1_w_down[l].astype(BF16), mix_norm[l].reshape(1, d), w[:, :k0].astype(BF16),
                    w[:, k0:k0 + mw].T.astype(BF16), w[:, k0 + mw:k0 + 2 * mw].T.astype(BF16),
                    w[:, k0 + 2 * mw:].astype(BF16))
        merge_w = (w_branch_a[l].astype(BF16), w_branch_b[l].astype(BF16), w_out[l].astype(BF16),
                   ffn2_norm[l].reshape(1, d), ffn2_w_gate[l].astype(BF16), ffn2_w_up[l].astype(BF16),
                   ffn2_w_down[l].astype(BF16), final_g)
        lb = lbs[l].reshape(1, nf)
        ng = hg_norm[l].reshape(1, HG_DV)

        x1p, zhp, mqp, ktp, vtp, gabp = _ffn_in(xp, *ffn_in_w)
        ya_p, st_p = _hgrn(zhp, jnp.zeros((bsz, HG_HEADS, HG_DK, HG_DV), F32), lb, ng, seq)
        yb_p = _moba_prompt(mqp, ktp, vtp, rel_bias)
        yp = _merge_ffn(x1p, ya_p, yb_p, gabp, *merge_w)

        x1s, zhs, mqs, kts, vts, gabs = _ffn_in(xs, *ffn_in_w)
        pad_tok = max(16, n_tok)
        zhs_b = jnp.pad(zhs.reshape(dbs, n_tok, n_hg), ((0, 0), (0, pad_tok - n_tok), (0, 0)))
        ya_s, st_s = _hgrn(zhs_b, state_hgrn[l], lb, ng, n_tok)
        ya_s = ya_s[:, :n_tok].reshape(1, dbs * n_tok, nv)
        kown = kts[0].T.reshape(dbs, n_tok, mw)
        vown = vts[0].T.reshape(dbs, n_tok, mw)
        yb_s = _moba_sample(mqs.reshape(dbs, n_tok, mw), kown, vown, ck_t, cv_t, page_table, rel_bias, l)
        ys = _merge_ffn(x1s, ya_s, yb_s.reshape(1, dbs * n_tok, mw), gabs, *merge_w)

        sp_l.append(st_p)
        ss_l.append(st_s)
        kp_l.append(jnp.swapaxes(ktp.reshape(bsz, MB_HEADS, MB_HD, seq), 2, 3))
        vp_l.append(jnp.swapaxes(vtp.reshape(bsz, MB_HEADS, MB_HD, seq), 2, 3))
        ks_l.append(kown.reshape(dbs, n_tok, MB_HEADS, MB_HD).transpose(0, 2, 1, 3))
        vs_l.append(vown.reshape(dbs, n_tok, MB_HEADS, MB_HD).transpose(0, 2, 1, 3))
    return (yp, ys.reshape(dbs, n_tok, d), jnp.stack(sp_l), jnp.stack(ss_l), jnp.stack(kp_l), jnp.stack(vp_l),
            jnp.stack(ks_l), jnp.stack(vs_l))
```

```python
import functools
import math

import numpy as np
import jax
import jax.numpy as jnp
from jax import lax
from jax.experimental import pallas as pl
from jax.experimental.pallas import tpu as pltpu

F32 = jnp.float32
BF16 = jnp.bfloat16

EPS = 1e-6
NEG = -1e30
HG_HEADS = 4
HG_DK = 128
HG_DV = 128
MB_HEADS = 8
MB_HD = 64
MB_BLOCK = 256
MB_TOPK = 3
PAGE_SIZE = 128
N_BUCKETS = 32
MAX_EXACT = N_BUCKETS // 2
MAX_DISTANCE = 128

V7X_VMEM_BYTES = 64 * 2**20
MXU_DEPTH = 256
VMEM_BIG = V7X_VMEM_BYTES - 8 * 2**20
VMEM_SMALL = 40 * 2**20

ROW_TILE = 256
ROW_TILE_MERGE = 512
HG_CHUNK = 128
HG_STEP = 512
HG_UNROLL = 4
HG_SHORT_SEQS = 8
HG_LONG_SEQS = 2
MB_QROWS = 256
PAGES_PER_STEP = 16
RING_SLOTS = 4

_NT = (((1,), (1,)), ((), ()))


def _dot(a, b):
    return jnp.dot(a, b, preferred_element_type=F32)


def _dot_nt(a, b):
    return lax.dot_general(a, b, _NT, preferred_element_type=F32)


def _split2(a):
    hi = a.astype(BF16)
    lo = (a - hi.astype(F32)).astype(BF16)
    return hi, lo


def _dot_precise(a, b):
    a1, a2 = _split2(a)
    b1, b2 = _split2(b)
    if 3 * a.shape[-1] <= MXU_DEPTH:
        return _dot(jnp.concatenate([a1, a1, a2], axis=-1), jnp.concatenate([b1, b2, b1], axis=0))
    return _dot(a1, b1) + _dot(a1, b2) + _dot(a2, b1)


def _rms(x, g):
    return x * lax.rsqrt(jnp.mean(x * x, axis=-1, keepdims=True) + EPS) * g


def _swiglu(h, wg_ref, wu_ref, wd_ref, ff_chunk):
    d_ff = wg_ref.shape[1]
    acc = None
    for c0 in range(0, d_ff, ff_chunk):
        c1 = min(c0 + ff_chunk, d_ff)
        a = _dot(h, wg_ref[:, c0:c1])
        u = _dot(h, wu_ref[:, c0:c1])
        act = (a * jax.nn.sigmoid(a) * u).astype(BF16)
        part = _dot(act, wd_ref[c0:c1, :])
        acc = part if acc is None else acc + part
    return acc


def _ff_chunk(d_ff):
    return -(-d_ff // (2 * MXU_DEPTH)) * MXU_DEPTH if d_ff % MXU_DEPTH == 0 else d_ff


def _const_spec(shape):
    nd = len(shape)
    return pl.BlockSpec(shape, lambda *_: (0,) * nd, pipeline_mode=pl.Buffered(1))


def _ffn_in_kernel(x_ref, n1_ref, wg_ref, wu_ref, wd_ref, nm_ref, wz_ref, wkt_ref, wvt_ref, wgab_ref,
                   x1_ref, zh_ref, mq_ref, kt_ref, vt_ref, gab_ref, *, ff_chunk, n_hg):
    x = x_ref[0]
    h = _rms(x, n1_ref[...]).astype(BF16)
    x1 = x + 0.5 * _swiglu(h, wg_ref, wu_ref, wd_ref, ff_chunk)
    x1_ref[0] = x1
    h2 = _rms(x1, nm_ref[...]).astype(BF16)
    z = _dot(h2, wz_ref[...])
    zh_ref[0] = z[:, :n_hg]
    mq_ref[0] = z[:, n_hg:]
    kt_ref[0] = _dot_nt(wkt_ref[...], h2)
    vt_ref[0] = _dot_nt(wvt_ref[...], h2)
    gab_ref[0] = _dot(h2, wgab_ref[...]).astype(BF16)


def _ffn_in(x, n1, wg, wu, wd, nm, wz, wkt, wvt, wgab):
    bsz, seq, d = x.shape
    tm = min(ROW_TILE, seq)
    assert seq % tm == 0
    n_z = wz.shape[1]
    n_hg = 2 * HG_HEADS * HG_DK + 2 * HG_HEADS * HG_DV
    n_kv = wkt.shape[0]
    n_gab = wgab.shape[1]
    row = lambda w: pl.BlockSpec((1, tm, w), lambda b, i: (b, i, 0))
    col = pl.BlockSpec((1, n_kv, tm), lambda b, i: (b, 0, i))
    outs = pl.pallas_call(
        functools.partial(_ffn_in_kernel, ff_chunk=_ff_chunk(wg.shape[1]), n_hg=n_hg),
        grid=(bsz, seq // tm),
        in_specs=[row(d), _const_spec(n1.shape), _const_spec(wg.shape), _const_spec(wu.shape),
                  _const_spec(wd.shape), _const_spec(nm.shape), _const_spec(wz.shape),
                  _const_spec(wkt.shape), _const_spec(wvt.shape), _const_spec(wgab.shape)],
        out_specs=[row(d), row(n_hg), row(n_z - n_hg), col, col, row(n_gab)],
        out_shape=[jax.ShapeDtypeStruct((bsz, seq, d), F32),
                   jax.ShapeDtypeStruct((bsz, seq, n_hg), F32),
                   jax.ShapeDtypeStruct((bsz, seq, n_z - n_hg), F32),
                   jax.ShapeDtypeStruct((bsz, n_kv, seq), F32),
                   jax.ShapeDtypeStruct((bsz, n_kv, seq), F32),
                   jax.ShapeDtypeStruct((bsz, seq, n_gab), BF16)],
        compiler_params=pltpu.CompilerParams(dimension_semantics=("parallel", "parallel"),
                                             vmem_limit_bytes=VMEM_BIG),
        name="ffn_in",
    )(x, n1, wg, wu, wd, nm, wz, wkt, wvt, wgab)
    return outs


def _merge_ffn_kernel(x1_ref, ya_ref, yb_ref, gab_ref, wa_ref, wb_ref, wo_ref, n2_ref,
                      wg_ref, wu_ref, wd_ref, nf_ref, y_ref, *, ff_chunk):
    d = x1_ref.shape[-1]
    gab = gab_ref[0].astype(F32)
    ma = _dot(ya_ref[0].astype(BF16), wa_ref[...])
    mb = _dot(yb_ref[0].astype(BF16), wb_ref[...])
    m = jax.nn.sigmoid(gab[:, :d]) * ma + jax.nn.sigmoid(gab[:, d:]) * mb
    x2 = x1_ref[0] + _dot(m.astype(BF16), wo_ref[...])
    h = _rms(x2, n2_ref[...]).astype(BF16)
    x3 = x2 + 0.5 * _swiglu(h, wg_ref, wu_ref, wd_ref, ff_chunk)
    y_ref[0] = _rms(x3, nf_ref[...])


def _merge_ffn(x1, ya, yb, gab, wa, wb, wo, n2, wg, wu, wd, nf):
    bsz, seq, d = x1.shape
    tm = min(ROW_TILE_MERGE, seq)
    assert seq % tm == 0
    row = lambda w: pl.BlockSpec((1, tm, w), lambda b, i: (b, i, 0))
    return pl.pallas_call(
        functools.partial(_merge_ffn_kernel, ff_chunk=_ff_chunk(wg.shape[1])),
        grid=(bsz, seq // tm),
        in_specs=[row(d), row(ya.shape[-1]), row(yb.shape[-1]), row(gab.shape[-1]),
                  _const_spec(wa.shape), _const_spec(wb.shape), _const_spec(wo.shape), _const_spec(n2.shape),
                  _const_spec(wg.shape), _const_spec(wu.shape), _const_spec(wd.shape), _const_spec(nf.shape)],
        out_specs=row(d),
        out_shape=jax.ShapeDtypeStruct((bsz, seq, d), F32),
        compiler_params=pltpu.CompilerParams(dimension_semantics=("parallel", "parallel"),
                                             vmem_limit_bytes=VMEM_BIG),
        name="merge_ffn",
    )(x1, ya, yb, gab, wa, wb, wo, n2, wg, wu, wd, nf)


def _hgrn_levels(chunk):
    levels, m = [], 1
    while m < chunk:
        levels.append(m)
        m *= 2
    return levels


HG_SPLIT = 2


def _hgrn_decay_matrix(chunk):
    t = np.arange(chunk)[:, None]
    u = np.arange(chunk)[None, :]
    mats = [u <= t]
    for m in _hgrn_levels(chunk):
        mid_q = t & ~(m - 1)
        q_side = ((t & m) != 0) & (u >= mid_q) & (u <= t)
        mid_k = (t & ~(2 * m - 1)) + m
        k_side = ((t & m) == 0) & (u > t) & (u < mid_k)
        mats.append(q_side | k_side)
    return np.tile(np.concatenate(mats, axis=0).astype(np.float32), (1, HG_SPLIT))


def _hgrn_kernel(zh_ref, s0_ref, lb_ref, ng_ref, dm_ref, ya_ref, sout_ref, st_ref,
                 *, chunk, n_inner, valid_len, n_seq):
    nh, dk, dv = HG_HEADS, HG_DK, HG_DV
    nf = nh * dk
    j = pl.program_id(1)

    @pl.when(j == 0)
    def _():
        for s in range(n_seq):
            for h in range(nh):
                st_ref[s, h] = s0_ref[s, h].T

    lb = lb_ref[...]
    ng = ng_ref[...]
    dmat = dm_ref[...]
    ti = lax.broadcasted_iota(jnp.int32, (chunk, chunk), 0)
    si = lax.broadcasted_iota(jnp.int32, (chunk, chunk), 1)
    xs = jnp.where(ti > si, jnp.bitwise_xor(ti, si), 0)
    levels = _hgrn_levels(chunk)

    def one_chunk(ci, carry):
        for s in range(n_seq):
            seq_chunk(s, pl.multiple_of(ci * chunk, chunk))
        return carry

    def seq_chunk(s, r0):
        zh = zh_ref[s, pl.ds(r0, chunk), :]
        hq, hf = zh[:, :nf], zh[:, nf:2 * nf]
        hv, hg = zh[:, 2 * nf:2 * nf + nh * dv], zh[:, 2 * nf + nh * dv:]
        f = lb + (1.0 - lb) * jax.nn.sigmoid(hf)
        logf = jnp.log2(f)
        kk = 1.0 - f
        if valid_len < chunk:
            live = lax.broadcasted_iota(jnp.int32, (chunk, 1), 0) < valid_len
            logf = jnp.where(live, logf, 0.0)
            kk = jnp.where(live, kk, 0.0)
        qf = hq * jax.nn.sigmoid(hq)
        pieces, rest = [], logf
        for _ in range(HG_SPLIT):
            piece = rest.astype(BF16)
            pieces.append(piece)
            rest = rest - piece.astype(F32)
        dec = _dot(dmat, jnp.concatenate(pieces, axis=0))
        outs = []
        for h in range(nh):
            ks = slice(h * dk, (h + 1) * dk)
            vs = slice(h * dv, (h + 1) * dv)
            q_h, k_h, v_h = qf[:, ks], kk[:, ks], hv[:, vs]
            b = dec[0:chunk, ks]
            after = b[chunk - 1:chunk, :] - b
            st = st_ref[s, h]
            o = _dot_nt((q_h * jnp.exp2(b)).astype(BF16), st.astype(BF16))
            att = None
            for li, m in enumerate(levels):
                w = jnp.exp2(dec[(1 + li) * chunk:(2 + li) * chunk, ks])
                a_l = _dot_nt((q_h * w).astype(BF16), (k_h * w).astype(BF16))
                att = a_l if att is None else jnp.where(xs >= m, a_l, att)
            diag = jnp.sum(q_h * k_h, axis=-1, keepdims=True)
            att = jnp.where(xs >= 1, att, 0.0)
            att = jnp.where(ti == si, diag, att)
            o = o + _dot(att.astype(BF16), v_h.astype(BF16))
            kd = (k_h * jnp.exp2(after)).astype(BF16)
            st_ref[s, h] = jnp.exp2(b[chunk - 1:chunk, :]) * st + _dot(v_h.T.astype(BF16), kd)
            o = o * lax.rsqrt(jnp.mean(o * o, axis=-1, keepdims=True) + EPS) * ng
            g_h = hg[:, vs]
            outs.append((o * (g_h * jax.nn.sigmoid(g_h))).astype(BF16))
        ya_ref[s, pl.ds(r0, chunk), :] = jnp.concatenate(outs, axis=-1)

    lax.fori_loop(0, n_inner, one_chunk, 0, unroll=min(HG_UNROLL, n_inner))

    @pl.when(j == pl.num_programs(1) - 1)
    def _():
        for s in range(n_seq):
            for h in range(nh):
                sout_ref[s, h] = st_ref[s, h].T


def _hgrn(zh, s0, lb, ng, valid_len):
    bsz, seq, width = zh.shape
    chunk = min(HG_CHUNK, seq)
    step = min(HG_STEP, seq)
    assert seq % step == 0 and step % chunk == 0 and chunk % 16 == 0
    dmat = jnp.asarray(_hgrn_decay_matrix(chunk), BF16)
    nv = HG_HEADS * HG_DV
    n_seq = math.gcd(bsz, HG_SHORT_SEQS if seq == chunk else HG_LONG_SEQS)
    st_spec = pl.BlockSpec((n_seq, HG_HEADS, HG_DK, HG_DV), lambda b, j: (b, 0, 0, 0))
    return pl.pallas_call(
        functools.partial(_hgrn_kernel, chunk=chunk, n_inner=step // chunk, valid_len=min(valid_len, chunk),
                          n_seq=n_seq),
        grid=(bsz // n_seq, seq // step),
        in_specs=[pl.BlockSpec((n_seq, step, width), lambda b, j: (b, j, 0)), st_spec,
                  _const_spec(lb.shape), _const_spec(ng.shape), _const_spec(dmat.shape)],
        out_specs=[pl.BlockSpec((n_seq, step, nv), lambda b, j: (b, j, 0)), st_spec],
        out_shape=[jax.ShapeDtypeStruct((bsz, seq, nv), BF16),
                   jax.ShapeDtypeStruct((bsz, HG_HEADS, HG_DK, HG_DV), F32)],
        scratch_shapes=[pltpu.VMEM((n_seq, HG_HEADS, HG_DV, HG_DK), F32)],
        compiler_params=pltpu.CompilerParams(dimension_semantics=("parallel", "arbitrary"),
                                             vmem_limit_bytes=VMEM_SMALL),
        name="hgrn",
    )(zh, s0, lb, ng, dmat)


def _rel_bucket(dist):
    n = jnp.maximum(dist, 0)
    nf = jnp.maximum(n, 1).astype(F32)
    large = MAX_EXACT + (jnp.log(nf / MAX_EXACT) / math.log(MAX_DISTANCE / MAX_EXACT)
                         * (N_BUCKETS - MAX_EXACT)).astype(jnp.int32)
    large = jnp.minimum(large, N_BUCKETS - 1)
    return jnp.where(n < MAX_EXACT, n, large)


def _top_mask(gs, lane, n_sel):
    mask = jnp.zeros(gs.shape, F32)
    lane_f = lane.astype(F32)
    cur = gs
    for _ in range(n_sel):
        best = jnp.max(cur, axis=-1, keepdims=True)
        idx = jnp.min(jnp.where(cur == best, lane_f, float(gs.shape[-1])), axis=-1, keepdims=True)
        hit = lane_f == idx
        mask = jnp.where(hit, 1.0, mask)
        cur = jnp.where(hit, -jnp.inf, cur)
    return mask


def _moba_prompt_kernel(th_ref, rb_ref, q_ref, kt_ref, vt_ref, o_ref, bias_sc, *, n_blocks):
    blk, hd = MB_BLOCK, MB_HD
    heads = q_ref.shape[-1] // hd
    seq = q_ref.shape[1]
    log2e = 1.0 / math.log(2.0)
    scale = hd ** -0.5 * log2e
    grp = pl.program_id(0)
    ti = lax.broadcasted_iota(jnp.int32, (blk, blk), 0)
    si = lax.broadcasted_iota(jnp.int32, (blk, blk), 1)
    causal = ti >= si

    @pl.when(pl.program_id(1) == 0)
    def _():
        for delta in range(2):
            d = jnp.maximum(delta * blk + ti - si, 0)
            for hl in range(heads):
                base = (grp * heads + hl) * N_BUCKETS
                far = rb_ref[base + N_BUCKETS - 1]
                val = jnp.full((blk, blk), far, F32)
                for bk in range(N_BUCKETS - 2, -1, -1):
                    val = jnp.where(d < th_ref[bk + 1], rb_ref[base + bk], val)
                bias_sc[delta, hl] = (val - far) * log2e

    row = lax.broadcasted_iota(jnp.int32, (hd, seq), 0)
    col_blk = lax.broadcasted_iota(jnp.int32, (hd, seq), 1) // blk
    indicator = jnp.where(row == col_blk, 1.0, 0.0).astype(BF16)
    ones_row = jnp.where(row == 0, 1.0, 0.0).astype(BF16)

    nb8 = -(-n_blocks // 8) * 8
    blk_idx = lax.broadcasted_iota(jnp.int32, (nb8, seq), 0)
    q_blk = lax.broadcasted_iota(jnp.int32, (nb8, seq), 1) // blk
    fully_past = blk_idx < q_blk

    q_all = q_ref[0]
    qs, ktx, vtx = [], [], []
    for hl in range(heads):
        q = q_all[:, hl * hd:(hl + 1) * hd]
        kt = kt_ref[0, hl * hd:(hl + 1) * hd, :]
        ktx.append(jnp.concatenate([kt.astype(BF16), indicator], axis=0))
        vtx.append(jnp.concatenate([vt_ref[0, hl * hd:(hl + 1) * hd, :].astype(BF16), ones_row], axis=0))
        if n_blocks > 1:
            lane_k = lax.broadcasted_iota(jnp.int32, (hd, 128), 1)
            kmean = jnp.zeros((hd, 128), F32)
            for n in range(n_blocks):
                ksum = jnp.sum(kt[:, n * blk:(n + 1) * blk], axis=-1, keepdims=True)
                kmean = jnp.where(lane_k == n, ksum * (1.0 / blk), kmean)
            gate = _dot_precise(q, kmean).T[:nb8]
            gate = jnp.where(fully_past, gate, -jnp.inf)
            rank = jnp.zeros((nb8, seq), F32)
            for r in range(1, nb8):
                other, other_idx = pltpu.roll(gate, r, 0), pltpu.roll(blk_idx, r, 0)
                beats = (other > gate) | ((other == gate) & (other_idx < blk_idx))
                rank = rank + jnp.where(beats, 1.0, 0.0)
            keep = (fully_past & (rank < MB_TOPK)) | (blk_idx >= q_blk)
            pen_t = jnp.concatenate([jnp.where(keep, 0.0, NEG), jnp.zeros((128 - nb8, seq), F32)], axis=0)
            pen = pen_t.T[:, :hd].astype(BF16)
        else:
            pen = jnp.zeros((seq, hd), BF16)
        qs.append(jnp.concatenate([(q * scale).astype(BF16), pen], axis=-1))

    qr = MB_QROWS
    for i, sub in [(i, sub) for i in range(n_blocks - 1, -1, -1) for sub in range(blk // qr)]:
        rows = slice(i * blk + sub * qr, i * blk + (sub + 1) * qr)
        in_blk = slice(sub * qr, (sub + 1) * qr)
        outs = []
        for hl in range(heads):
            s_all = _dot(qs[hl][rows], ktx[hl][:, :(i + 1) * blk])
            tiles = []
            for jb in range(i + 1):
                s = s_all[:, jb * blk:(jb + 1) * blk]
                if jb == i:
                    s = jnp.where(causal[in_blk], s + bias_sc[0, hl, in_blk, :], NEG)
                elif jb == i - 1:
                    s = s + bias_sc[1, hl, in_blk, :]
                tiles.append(s)
            mx = tiles[0]
            for s in tiles[1:]:
                mx = jnp.maximum(mx, s)
            mx = jnp.max(mx, axis=-1, keepdims=True)
            p = jnp.concatenate([jnp.exp2(s - mx).astype(BF16) for s in tiles], axis=-1)
            acc = _dot_nt(p, vtx[hl][:, :(i + 1) * blk])
            outs.append((acc[:, :hd] / acc[:, hd:hd + 1]).astype(BF16))
        o_ref[0, rows, :] = jnp.concatenate(outs, axis=-1)


def _bucket_thresholds():
    buckets = _rel_bucket(jnp.arange(MAX_DISTANCE + 1))
    return jnp.sum(buckets[None, :] < jnp.arange(N_BUCKETS + 1)[:, None], axis=1).astype(jnp.int32)


def _moba_prompt(mq, kt, vt, rel_bias):
    bsz, seq, width = mq.shape
    assert seq % MB_BLOCK == 0 and MB_BLOCK >= MAX_DISTANCE and seq // MB_BLOCK <= MB_HD
    hp = 128 // MB_HD
    n_groups = width // 128
    smem = pl.BlockSpec(memory_space=pltpu.SMEM)
    return pl.pallas_call(
        functools.partial(_moba_prompt_kernel, n_blocks=seq // MB_BLOCK),
        grid=(n_groups, bsz),
        in_specs=[smem, smem,
                  pl.BlockSpec((1, seq, 128), lambda g, b: (b, 0, g)),
                  pl.BlockSpec((1, 128, seq), lambda g, b: (b, g, 0)),
                  pl.BlockSpec((1, 128, seq), lambda g, b: (b, g, 0))],
        out_specs=pl.BlockSpec((1, seq, 128), lambda g, b: (b, 0, g)),
        out_shape=jax.ShapeDtypeStruct((bsz, seq, width), BF16),
        scratch_shapes=[pltpu.VMEM((2, hp, MB_BLOCK, MB_BLOCK), F32)],
        compiler_params=pltpu.CompilerParams(dimension_semantics=("parallel", "arbitrary"),
                                             vmem_limit_bytes=VMEM_SMALL),
        name="moba_prompt",
    )(_bucket_thresholds(), rel_bias.T.reshape(-1).astype(F32), mq, kt, vt)


def _moba_sample_kernel(pt_ref, q_ref, kown_ref, vown_ref, bfar_ref, blast_ref, bown_ref, ck_ref, cv_ref,
                        o_ref, buf, sem, s_all, *, n_pages, pps, layer):
    nh, hd = MB_HEADS, MB_HD
    n_tok = q_ref.shape[1]
    rows = nh * n_tok
    width = nh * hd
    ppb = MB_BLOCK // PAGE_SIZE
    n_full = n_pages // ppb
    gk = n_pages // pps
    n_groups = 2 * gk
    n_slots = buf.shape[0]
    ahead = n_slots - 1
    seq_id = pl.program_id(0)
    n_seq = pl.num_programs(0)

    def group_copies(slot, cache_ref, page_of):
        return [pltpu.make_async_copy(cache_ref.at[layer, page_of(jj)], buf.at[slot, jj], sem.at[slot])
                for jj in range(pps)]

    def start_group(seq, loc):
        slot = (seq * n_groups + loc) % n_slots

        @pl.when(loc < gk)
        def _():
            for cp in group_copies(slot, ck_ref, lambda jj: pt_ref[seq * n_pages + loc * pps + jj]):
                cp.start()

        @pl.when(loc >= gk)
        def _():
            for cp in group_copies(slot, cv_ref, lambda jj: pt_ref[seq * n_pages + (loc - gk) * pps + jj]):
                cp.start()

    def prefetch(loc):
        nxt = loc + ahead

        @pl.when(nxt < n_groups)
        def _():
            start_group(seq_id, nxt)

        @pl.when((nxt >= n_groups) & (seq_id + 1 < n_seq))
        def _():
            start_group(seq_id + 1, nxt - n_groups)

    def wait_group(loc):
        slot = (seq_id * n_groups + loc) % n_slots
        for cp in group_copies(slot, ck_ref, lambda jj: 0):
            cp.wait()
        return slot

    @pl.when(seq_id == 0)
    def _():
        for loc in range(ahead):
            start_group(seq_id, jnp.int32(loc))

    q = q_ref[0]
    rh = lax.broadcasted_iota(jnp.int32, (rows, width), 0) // n_tok
    ch = lax.broadcasted_iota(jnp.int32, (rows, width), 1) // hd
    own_head = rh == ch
    q_rep = jnp.concatenate([q] * nh, axis=0)
    qbd = jnp.where(own_head, q_rep * (hd ** -0.5), 0.0).astype(BF16)
    lane = lax.broadcasted_iota(jnp.int32, (rows, PAGE_SIZE), 1)

    lane_w = lax.broadcasted_iota(jnp.int32, (width, PAGE_SIZE), 1)

    def key_group(g, carry):
        pm, ms = carry
        slot = wait_group(g)
        prefetch(g)
        for jb in range(pps // ppb):
            blk_n = g * (pps // ppb) + jb
            kps = [buf[slot, jb * ppb + pj].reshape(width, PAGE_SIZE) for pj in range(ppb)]
            s_blk = _dot(qbd, jnp.concatenate([kp.astype(BF16) for kp in kps], axis=-1))
            s_max, ksum = None, None
            for pj, kp in enumerate(kps):
                s = s_blk[:, pj * PAGE_SIZE:(pj + 1) * PAGE_SIZE]
                s_all[g * pps + jb * ppb + pj] = s
                s_max = s if s_max is None else jnp.maximum(s_max, s)
                ksum = kp if ksum is None else ksum + kp
            pm = jnp.where(lane == blk_n, jnp.max(s_max, axis=-1, keepdims=True), pm)
            ms = jnp.where(lane_w == blk_n, jnp.sum(ksum, axis=-1, keepdims=True), ms)
        return pm, ms

    pm, ms = lax.fori_loop(0, gk, key_group, (jnp.full((rows, PAGE_SIZE), -jnp.inf, F32),
                                              jnp.zeros((width, PAGE_SIZE), F32)))

    q_bd32 = jnp.where(own_head, q_rep, 0.0)
    gs = _dot_precise(q_bd32, ms * (1.0 / MB_BLOCK))
    gs = jnp.where(lane < n_full, gs, -jnp.inf)
    sel = _top_mask(gs, lane, min(MB_TOPK, n_full))
    sel = sel > 0.5
    bfar = bfar_ref[...]
    s_max = None
    for pj in range(ppb):
        page = n_pages - ppb + pj
        s = s_all[page] + (blast_ref[:, pj * PAGE_SIZE:(pj + 1) * PAGE_SIZE] - bfar)
        s_all[page] = s
        s_max = s if s_max is None else jnp.maximum(s_max, s)
    pm = jnp.where(lane == n_full - 1, jnp.max(s_max, axis=-1, keepdims=True), pm)

    pad = jnp.zeros((PAGE_SIZE - n_tok, width), F32)
    kown = jnp.concatenate([kown_ref[0], pad], axis=0).astype(BF16)
    vown = jnp.concatenate([vown_ref[0], pad], axis=0).astype(BF16)
    tq = lax.broadcasted_iota(jnp.int32, (rows, PAGE_SIZE), 0) % n_tok
    s_own = jnp.where(lane <= tq, _dot_nt(qbd, kown) + (bown_ref[...] - bfar), -jnp.inf)
    mx = jnp.max(jnp.maximum(jnp.where(sel, pm, -jnp.inf), s_own), axis=-1, keepdims=True)
    shift = jnp.where(sel, mx, jnp.inf)
    p_own = jnp.exp(s_own - mx)

    def value_group(g, carry):
        den, acc = carry
        slot = wait_group(gk + g)
        prefetch(gk + g)
        probs = []
        for jb in range(pps // ppb):
            blk_n = g * (pps // ppb) + jb
            sh = jnp.min(jnp.where(lane == blk_n, shift, jnp.inf), axis=-1, keepdims=True)
            for pj in range(ppb):
                p = jnp.exp(s_all[g * pps + jb * ppb + pj] - sh)
                den = den + p
                probs.append(p.astype(BF16))
        v_grp = jnp.concatenate([buf[slot, jj].reshape(width, PAGE_SIZE).astype(BF16) for jj in range(pps)], axis=-1)
        acc = acc + _dot_nt(jnp.concatenate(probs, axis=-1), v_grp)
        return den, acc

    den, acc = lax.fori_loop(0, gk, value_group, (p_own, _dot(p_own.astype(BF16), vown)))
    o = jnp.where(own_head, acc / jnp.sum(den, axis=-1, keepdims=True), 0.0)
    out = o[0:n_tok]
    for h in range(1, nh):
        out = out + o[h * n_tok:(h + 1) * n_tok]
    o_ref[0] = out


def _moba_sample(mq, kown, vown, ck_t, cv_t, page_table, rel_bias, layer):
    dbs, n_tok, width = mq.shape
    n_pages = page_table.shape[1]
    past = n_pages * PAGE_SIZE
    ppb = MB_BLOCK // PAGE_SIZE
    pps = PAGES_PER_STEP
    assert past % MB_BLOCK == 0 and n_pages % pps == 0 and pps % ppb == 0 and n_pages // ppb >= MB_TOPK
    assert n_tok <= MAX_EXACT and MB_BLOCK >= MAX_DISTANCE and n_tok % 8 == 0 and n_pages <= PAGE_SIZE
    rows = MB_HEADS * n_tok
    gk = n_pages // pps
    t = jnp.arange(n_tok)
    d_last = (t[:, None] + 1) + jnp.arange(MB_BLOCK)[::-1][None, :]
    blast = rel_bias[_rel_bucket(d_last)].transpose(2, 0, 1).reshape(rows, MB_BLOCK)
    bfar = jnp.broadcast_to(rel_bias[_rel_bucket(jnp.asarray(MB_BLOCK + 1))][:, None, None],
                            (MB_HEADS, n_tok, PAGE_SIZE)).reshape(rows, PAGE_SIZE)
    d_own = t[:, None] - jnp.arange(PAGE_SIZE)[None, :]
    bown = rel_bias[_rel_bucket(d_own)].transpose(2, 0, 1).reshape(rows, PAGE_SIZE)

    tok_spec = pl.BlockSpec((1, n_tok, width), lambda b, pt: (b, 0, 0))
    full = lambda a: pl.BlockSpec(a.shape, lambda b, pt: (0,) * a.ndim)
    hbm = pl.BlockSpec(memory_space=pl.ANY)
    grid_spec = pltpu.PrefetchScalarGridSpec(
        num_scalar_prefetch=1,
        grid=(dbs,),
        in_specs=[tok_spec, tok_spec, tok_spec, full(bfar), full(blast), full(bown), hbm, hbm],
        out_specs=tok_spec,
        scratch_shapes=[pltpu.VMEM((RING_SLOTS, pps, MB_HEADS, MB_HD, PAGE_SIZE), F32),
                        pltpu.SemaphoreType.DMA((RING_SLOTS,)),
                        pltpu.VMEM((n_pages, rows, PAGE_SIZE), F32)],
    )
    return pl.pallas_call(
        functools.partial(_moba_sample_kernel, n_pages=n_pages, pps=pps, layer=layer),
        grid_spec=grid_spec,
        out_shape=jax.ShapeDtypeStruct((dbs, n_tok, width), F32),
        compiler_params=pltpu.CompilerParams(dimension_semantics=("arbitrary",),
                                             vmem_limit_bytes=VMEM_SMALL),
        name="moba_sample",
    )(page_table.reshape(-1), mq, kown, vown, bfar, blast, bown, ck_t, cv_t)


def kernel(x_prompt, x_sample, state_hgrn, cache_k, cache_v, page_table, ffn1_norm, ffn1_w_gate, ffn1_w_up,
           ffn1_w_down, mix_norm, w_in, hg_lb, hg_norm, rel_bias, w_branch_a, w_branch_b, w_out, ffn2_norm,
           ffn2_w_gate, ffn2_w_up, ffn2_w_down, final_norm):
    depth = w_in.shape[0]
    assert depth == 1, "the final norm is fused into the (single) layer's last kernel"
    bsz, seq, d = x_prompt.shape
    dbs, n_tok, _ = x_sample.shape
    nf, nv, mw = HG_HEADS * HG_DK, HG_HEADS * HG_DV, MB_HEADS * MB_HD
    n_hg = 2 * nf + 2 * nv
    assert w_in.shape[2] == n_hg + 3 * mw + 2 * d

    lbs = jnp.cumsum(jax.nn.softmax(hg_lb.astype(F32), axis=0), axis=0)
    ck_t = jnp.swapaxes(cache_k, 3, 4)
    cv_t = jnp.swapaxes(cache_v, 3, 4)
    final_g = final_norm.reshape(1, d)

    xp = x_prompt
    xs = x_sample.reshape(1, dbs * n_tok, d)
    sp_l, ss_l, kp_l, vp_l, ks_l, vs_l = [], [], [], [], [], []
    yp = ys = None
    for l in range(depth):
        w = w_in[l]
        k0 = n_hg + mw
        ffn_in_w = (ffn1_norm[l].reshape(1, d), ffn1_w_gate[l].astype(BF16), ffn1_w_up[l].astype(BF16),
                    ffn1_w_down[l].astype(BF16), mix_norm[l].reshape(1, d), w[:, :k0].astype(BF16),
                    w[:, k0:k0 + mw].T.astype(BF16), w[:, k0 + mw:k0 + 2 * mw].T.astype(BF16),
                    w[:, k0 + 2 * mw:].astype(BF16))
        merge_w = (w_branch_a[l].astype(BF16), w_branch_b[l].astype(BF16), w_out[l].astype(BF16),
                   ffn2_norm[l].reshape(1, d), ffn2_w_gate[l].astype(BF16), ffn2_w_up[l].astype(BF16),
                   ffn2_w_down[l].astype(BF16), final_g)
        lb = lbs[l].reshape(1, nf)
        ng = hg_norm[l].reshape(1, HG_DV)

        x1p, zhp, mqp, ktp, vtp, gabp = _ffn_in(xp, *ffn_in_w)
        ya_p, st_p = _hgrn(zhp, jnp.zeros((bsz, HG_HEADS, HG_DK, HG_DV), F32), lb, ng, seq)
        yb_p = _moba_prompt(mqp, ktp, vtp, rel_bias)
        yp = _merge_ffn(x1p, ya_p, yb_p, gabp, *merge_w)

        x1s, zhs, mqs, kts, vts, gabs = _ffn_in(xs, *ffn_in_w)
        pad_tok = max(16, n_tok)
        zhs_b = jnp.pad(zhs.reshape(dbs, n_tok, n_hg), ((0, 0), (0, pad_tok - n_tok), (0, 0)))
        ya_s, st_s = _hgrn(zhs_b, state_hgrn[l], lb, ng, n_tok)
        ya_s = ya_s[:, :n_tok].reshape(1, dbs * n_tok, nv)
        kown = kts[0].T.reshape(dbs, n_tok, mw)
        vown = vts[0].T.reshape(dbs, n_tok, mw)
        yb_s = _moba_sample(mqs.reshape(dbs, n_tok, mw), kown, vown, ck_t, cv_t, page_table, rel_bias, l)
        ys = _merge_ffn(x1s, ya_s, yb_s.reshape(1, dbs * n_tok, mw), gabs, *merge_w)

        sp_l.append(st_p)
        ss_l.append(st_s)
        kp_l.append(jnp.swapaxes(ktp.reshape(bsz, MB_HEADS, MB_HD, seq), 2, 3))
        vp_l.append(jnp.swapaxes(vtp.reshape(bsz, MB_HEADS, MB_HD, seq), 2, 3))
        ks_l.append(kown.reshape(dbs, n_tok, MB_HEADS, MB_HD).transpose(0, 2, 1, 3))
        vs_l.append(vown.reshape(dbs, n_tok, MB_HEADS, MB_HD).transpose(0, 2, 1, 3))
    return (yp, ys.reshape(dbs, n_tok, d), jnp.stack(sp_l), jnp.stack(ss_l), jnp.stack(kp_l), jnp.stack(vp_l),
            jnp.stack(ks_l), jnp.stack(vs_l))
```
